```python
import math
import jax
import jax.numpy as jnp
from jax import lax
import numpy as np

D_MODEL = 1024
BATCH = 16
SEQ = 4096
DEPTH = 4

N_MIXERS = 3
N_GDN = (DEPTH + 2) // 3
N_SB = (DEPTH + 1) // 3
N_DSA = DEPTH // 3

D_FF = 2816
NORM_EPS = 1e-6
ROPE_THETA = 500000.0
Q_BLOCK = 128

GDN_HEADS = 8
GDN_HEAD_DIM = 128
GDN_DIM = GDN_HEADS * GDN_HEAD_DIM
GDN_CONV = 4
GDN_CHUNK = 64
GDN_IN = 4 * GDN_DIM + 2 * GDN_HEADS

SB_HEADS = 8
SB_HEAD_DIM = 128
SB_DIM = SB_HEADS * SB_HEAD_DIM

DSA_HEADS = 8
DSA_Q_RANK = 384
DSA_KV_RANK = 256
DSA_ROPE_DIM = 32
DSA_NOPE_DIM = 96
DSA_QK_DIM = DSA_ROPE_DIM + DSA_NOPE_DIM
DSA_V_DIM = 128
DSA_IDX_HEADS = 8
DSA_IDX_DIM = 64
DSA_TOPK_MAX = 256
DSA_IN = DSA_Q_RANK + DSA_KV_RANK + DSA_ROPE_DIM + DSA_IDX_DIM + DSA_IDX_HEADS

kernel_name = 'hybrid_gdn_stickbreak_dsa_macaron'


def rmsnorm(x, g):
    xf = x.astype(jnp.float32)
    y = xf * lax.rsqrt(jnp.mean(xf * xf, axis=-1, keepdims=True) + NORM_EPS)
    return (y * g.astype(jnp.float32)).astype(x.dtype)


def l2norm(x):
    return x * lax.rsqrt(jnp.sum(x * x, axis=-1, keepdims=True) + NORM_EPS)


def rope(x, pos):
    r = x.shape[-1]
    half = r // 2
    inv_freq = ROPE_THETA ** (-jnp.arange(half, dtype=jnp.float32) * (2.0 / r))
    ang = pos.astype(jnp.float32)[:, None] * inv_freq[None, :]
    shape = (pos.shape[0],) + (1,) * (x.ndim - 3) + (half,)
    cos = jnp.cos(ang).reshape(shape)
    sin = jnp.sin(ang).reshape(shape)
    xf = x.astype(jnp.float32)
    x1, x2 = xf[..., :half], xf[..., half:]
    return jnp.concatenate([x1 * cos - x2 * sin, x2 * cos + x1 * sin], axis=-1).astype(x.dtype)


def partial_rope(x, pos):
    r = x.shape[-1] // 4
    return jnp.concatenate([rope(x[..., :r], pos), x[..., r:]], axis=-1)


def swiglu(h, w_gu, w_down):
    gate, up = jnp.split(h @ w_gu, 2, axis=-1)
    return (jax.nn.silu(gate) * up) @ w_down


def causal_depthwise_conv(x, w):
    kw, ch = w.shape
    return lax.conv_general_dilated(
        x, w[:, None, :].astype(x.dtype), window_strides=(1,), padding=[(kw - 1, 0)],
        dimension_numbers=('NWC', 'WIO', 'NWC'), feature_group_count=ch)


def to_blocks(t, block):
    b, s = t.shape[:2]
    return jnp.swapaxes(t.reshape((b, s // block, block) + t.shape[2:]), 0, 1)


def from_blocks(t):
    nb, b, blk = t.shape[:3]
    return jnp.swapaxes(t, 0, 1).reshape((b, nb * blk) + t.shape[3:])


def chunk_gated_delta_rule(q, k, v, g, beta):
    bsz, seq, nh, dk = q.shape
    dv = v.shape[-1]
    c = GDN_CHUNK
    n = seq // c

    def chunks(t):
        return jnp.moveaxis(t.reshape((bsz, n, c, nh) + t.shape[3:]), 3, 1)

    q, k, v, g, beta = chunks(q), chunks(k), chunks(v), chunks(g), chunks(beta)
    g_cum = jnp.cumsum(g, axis=-1)
    incl = jnp.tril(jnp.ones((c, c), dtype=bool))
    strict = jnp.tril(jnp.ones((c, c), dtype=bool), -1)
    seg = g_cum[..., :, None] - g_cum[..., None, :]
    decay = jnp.where(incl, jnp.exp(jnp.where(incl, seg, 0.0)), 0.0)
    kk = jnp.einsum('bhnid,bhnjd->bhnij', k, k)
    lower = jnp.where(strict, beta[..., :, None] * kk * decay, 0.0)
    eye = jnp.eye(c, dtype=jnp.float32)
    rhs = jnp.concatenate([v * beta[..., None], k * (beta * jnp.exp(g_cum))[..., None]], axis=-1)
    sol = lax.linalg.triangular_solve(lower + eye, rhs, left_side=True, lower=True, unit_diagonal=True)
    u_in, w = sol[..., :dv], sol[..., dv:]
    qk = jnp.einsum('bhnid,bhnjd->bhnij', q, k) * decay
    q_dec = q * jnp.exp(g_cum)[..., None]
    k_dec = k * jnp.exp(g_cum[..., -1:] - g_cum)[..., None]
    g_tot = jnp.exp(g_cum[..., -1])
    xs = tuple(jnp.moveaxis(t, 2, 0) for t in (u_in, w, qk, q_dec, k_dec, g_tot))

    def step(state, inp):
        u_c, w_c, qk_c, qd_c, kd_c, gt_c = inp
        u = u_c - jnp.einsum('bhck,bhkv->bhcv', w_c, state)
        o = jnp.einsum('bhck,bhkv->bhcv', qd_c, state) + jnp.einsum('bhcj,bhjv->bhcv', qk_c, u)
        state = state * gt_c[..., None, None] + jnp.einsum('bhck,bhcv->bhkv', kd_c, u)
        return state, o

    state0 = jnp.zeros((bsz, nh, dk, dv), jnp.float32)
    _, o = lax.scan(step, state0, xs)
    return jnp.transpose(o, (1, 0, 3, 2, 4)).reshape(bsz, seq, nh, dv)


def gated_deltanet(h, w_in, conv_w, a_log, dt_bias, norm_g, w_out):
    bsz, seq, _ = h.shape
    proj = h @ w_in
    qkv = jax.nn.silu(causal_depthwise_conv(proj[..., :3 * GDN_DIM], conv_w))
    z = proj[..., 3 * GDN_DIM:4 * GDN_DIM]
    b_logit = proj[..., 4 * GDN_DIM:4 * GDN_DIM + GDN_HEADS].astype(jnp.float32)
    a = proj[..., 4 * GDN_DIM + GDN_HEADS:].astype(jnp.float32)

    def heads(t):
        return t.reshape(bsz, seq, GDN_HEADS, GDN_HEAD_DIM).astype(jnp.float32)

    q, k, v = (heads(t) for t in jnp.split(qkv, 3, axis=-1))
    q = l2norm(q) * GDN_HEAD_DIM ** -0.5
    k = l2norm(k)
    beta = jax.nn.sigmoid(b_logit)
    g = -jnp.exp(a_log.astype(jnp.float32)) * jax.nn.softplus(a + dt_bias.astype(jnp.float32))
    o = chunk_gated_delta_rule(q, k, v, g, beta)
    o = rmsnorm(o, norm_g) * jax.nn.silu(heads(z))
    return o.reshape(bsz, seq, GDN_DIM).astype(h.dtype) @ w_out


def stick_breaking_attention(h, w_in, w_out):
    bsz, seq, _ = h.shape
    q, k, v = (t.reshape(bsz, seq, SB_HEADS, SB_HEAD_DIM).astype(jnp.float32)
               for t in jnp.split(h @ w_in, 3, axis=-1))
    scale = SB_HEAD_DIM ** -0.5
    key_pos = jnp.arange(seq)

    def block(args):
        q_blk, t0 = args
        z = jnp.einsum('bqhd,bkhd->bhqk', q_blk, k) * scale
        q_pos = t0 + jnp.arange(Q_BLOCK)
        earlier = key_pos[None, :] < q_pos[:, None]
        log_beta = jax.nn.log_sigmoid(z)
        log_fail = jnp.where(earlier, log_beta - z, 0.0)
        log_a = log_beta + lax.cumsum(log_fail, axis=3, reverse=True) - log_fail
        att = jnp.where(earlier, jnp.exp(log_a), 0.0)
        return jnp.einsum('bhqk,bkhd->bqhd', att, v)

    starts = jnp.arange(seq // Q_BLOCK, dtype=jnp.int32) * Q_BLOCK
    o = from_blocks(lax.map(block, (to_blocks(q, Q_BLOCK), starts)))
    return o.reshape(bsz, seq, SB_DIM).astype(h.dtype) @ w_out


def dsa_sparse_attention(h, w_in, cq_g, ckv_g, kidx_g, w_uq, w_qidx, w_uk, w_uv, w_out):
    bsz, seq, _ = h.shape
    topk = min(DSA_TOPK_MAX, seq // 4)
    pos = jnp.arange(seq)
    proj = h @ w_in
    o1 = DSA_Q_RANK
    o2 = o1 + DSA_KV_RANK
    o3 = o2 + DSA_ROPE_DIM
    o4 = o3 + DSA_IDX_DIM
    c_q = rmsnorm(proj[..., :o1], cq_g)
    c_kv = rmsnorm(proj[..., o1:o2], ckv_g)
    k_rope = rope(proj[..., o2:o3], pos)
    k_idx = partial_rope(rmsnorm(proj[..., o3:o4], kidx_g), pos).astype(jnp.float32)
    w_idx = proj[..., o4:].astype(jnp.float32) * (DSA_IDX_HEADS ** -0.5 * DSA_IDX_DIM ** -0.5)
    q = (c_q @ w_uq).reshape(bsz, seq, DSA_HEADS, DSA_QK_DIM)
    q_rope = rope(q[..., :DSA_ROPE_DIM], pos)
    q_nope = q[..., DSA_ROPE_DIM:]
    q_idx = partial_rope((c_q @ w_qidx).reshape(bsz, seq, DSA_IDX_HEADS, DSA_IDX_DIM), pos)
    q_abs = jnp.einsum('bshn,chn->bshc', q_nope, w_uk.reshape(DSA_KV_RANK, DSA_HEADS, DSA_NOPE_DIM))
    q_lat = jnp.concatenate([q_abs, q_rope], axis=-1).astype(jnp.float32)
    k_lat = jnp.concatenate([c_kv, k_rope], axis=-1).astype(jnp.float32)
    scale = DSA_QK_DIM ** -0.5
    batch_ix = jnp.arange(bsz)[:, None, None]

    def block(args):
        qi, wi, ql, t0 = args
        q_pos = t0 + jnp.arange(Q_BLOCK)
        logits = jnp.einsum('bqhd,bkd->bqhk', qi, k_idx)
        score = jnp.einsum('bqh,bqhk->bqk', wi, jax.nn.relu(logits))
        score = jnp.where(pos[None, None, :] <= q_pos[None, :, None], score, -jnp.inf)
        _, idx = lax.top_k(score, topk)
        sel = k_lat[batch_ix, idx]
        att = jnp.einsum('bqhr,bqkr->bqhk', ql, sel) * scale
        att = jnp.where((idx <= q_pos[None, :, None])[:, :, None, :], att, -jnp.inf)
        p = jax.nn.softmax(att, axis=-1)
        return jnp.einsum('bqhk,bqkc->bqhc', p, sel[..., :DSA_KV_RANK])

    starts = jnp.arange(seq // Q_BLOCK, dtype=jnp.int32) * Q_BLOCK
    o_lat = from_blocks(lax.map(block, (to_blocks(q_idx.astype(jnp.float32), Q_BLOCK),
                                        to_blocks(w_idx, Q_BLOCK),
                                        to_blocks(q_lat, Q_BLOCK), starts)))
    o = jnp.einsum('bshc,chv->bshv', o_lat,
                   w_uv.reshape(DSA_KV_RANK, DSA_HEADS, DSA_V_DIM).astype(jnp.float32))
    return o.reshape(bsz, seq, DSA_HEADS * DSA_V_DIM).astype(h.dtype) @ w_out


def setup_inputs(seed: int = 0) -> dict:
    key = jax.random.key(seed)
    ks = jax.random.split(key, 26)
    f32 = jnp.float32

    def dense(k, shape):
        return jax.random.normal(k, shape, f32) * shape[-2] ** -0.5

    def gain(k, shape):
        return 1.0 + 0.02 * jax.random.normal(k, shape, f32)

    x = jax.random.normal(ks[0], (BATCH, SEQ, D_MODEL), f32)
    dt = jnp.exp(jax.random.uniform(ks[12], (N_GDN, GDN_HEADS), f32, math.log(1e-3), math.log(1e-1)))
    return {
        'x': x,
        'ffn1_norm': gain(ks[1], (DEPTH, D_MODEL)),
        'ffn1_w_gu': dense(ks[2], (DEPTH, D_MODEL, 2 * D_FF)),
        'ffn1_w_down': dense(ks[3], (DEPTH, D_FF, D_MODEL)),
        'mix_norm': gain(ks[4], (DEPTH, D_MODEL)),
        'ffn2_norm': gain(ks[5], (DEPTH, D_MODEL)),
        'ffn2_w_gu': dense(ks[6], (DEPTH, D_MODEL, 2 * D_FF)),
        'ffn2_w_down': dense(ks[7], (DEPTH, D_FF, D_MODEL)),
        'gdn_w_in': dense(ks[8], (N_GDN, D_MODEL, GDN_IN)),
        'gdn_conv': jax.random.normal(ks[10], (N_GDN, GDN_CONV, 3 * GDN_DIM), f32) * GDN_CONV ** -0.5,
        'gdn_a_log': jnp.log(jax.random.uniform(ks[11], (N_GDN, GDN_HEADS), f32, 1.0, 16.0)),
        'gdn_dt_bias': dt + jnp.log(-jnp.expm1(-dt)),
        'gdn_norm': gain(ks[13], (N_GDN, GDN_HEAD_DIM)),
        'gdn_w_out': dense(ks[9], (N_GDN, GDN_DIM, D_MODEL)),
        'sb_w_in': dense(ks[14], (N_SB, D_MODEL, 3 * SB_DIM)),
        'sb_w_out': dense(ks[15], (N_SB, SB_DIM, D_MODEL)),
        'dsa_w_in': dense(ks[16], (N_DSA, D_MODEL, DSA_IN)),
        'dsa_cq_norm': gain(ks[17], (N_DSA, DSA_Q_RANK)),
        'dsa_ckv_norm': gain(ks[18], (N_DSA, DSA_KV_RANK)),
        'dsa_kidx_norm': gain(ks[19], (N_DSA, DSA_IDX_DIM)),
        'dsa_w_uq': dense(ks[20], (N_DSA, DSA_Q_RANK, DSA_HEADS * DSA_QK_DIM)),
        'dsa_w_qidx': dense(ks[21], (N_DSA, DSA_Q_RANK, DSA_IDX_HEADS * DSA_IDX_DIM)),
        'dsa_w_uk': dense(ks[22], (N_DSA, DSA_KV_RANK, DSA_HEADS * DSA_NOPE_DIM)),
        'dsa_w_uv': dense(ks[23], (N_DSA, DSA_KV_RANK, DSA_HEADS * DSA_V_DIM)),
        'dsa_w_out': dense(ks[24], (N_DSA, DSA_HEADS * DSA_V_DIM, D_MODEL)),
        'final_norm': gain(ks[25], (D_MODEL,)),
    }


def reference(x, ffn1_norm, ffn1_w_gu, ffn1_w_down, mix_norm, ffn2_norm, ffn2_w_gu, ffn2_w_down,
              gdn_w_in, gdn_conv, gdn_a_log, gdn_dt_bias, gdn_norm, gdn_w_out,
              sb_w_in, sb_w_out,
              dsa_w_in, dsa_cq_norm, dsa_ckv_norm, dsa_kidx_norm, dsa_w_uq, dsa_w_qidx,
              dsa_w_uk, dsa_w_uv, dsa_w_out, final_norm):
    for i in range(DEPTH):
        kind, j = i % N_MIXERS, i // N_MIXERS
        x = x + 0.5 * swiglu(rmsnorm(x, ffn1_norm[i]), ffn1_w_gu[i], ffn1_w_down[i])
        h = rmsnorm(x, mix_norm[i])
        if kind == 0:
            m = gated_deltanet(h, gdn_w_in[j], gdn_conv[j], gdn_a_log[j], gdn_dt_bias[j],
                               gdn_norm[j], gdn_w_out[j])
        elif kind == 1:
            m = stick_breaking_attention(h, sb_w_in[j], sb_w_out[j])
        else:
            m = dsa_sparse_attention(h, dsa_w_in[j], dsa_cq_norm[j], dsa_ckv_norm[j],
                                     dsa_kidx_norm[j], dsa_w_uq[j], dsa_w_qidx[j],
                                     dsa_w_uk[j], dsa_w_uv[j], dsa_w_out[j])
        x = x + m.astype(x.dtype)
        x = x + 0.5 * swiglu(rmsnorm(x, ffn2_norm[i]), ffn2_w_gu[i], ffn2_w_down[i])
    return rmsnorm(x, final_norm)
```

```python
import functools
import math

import jax
import jax.numpy as jnp
from jax import lax
from jax.experimental import pallas as pl
from jax.experimental.pallas import tpu as pltpu

F32 = jnp.float32
BF16 = jnp.bfloat16

NORM_EPS = 1e-6
ROPE_THETA = 500000.0
V7X_VMEM_LIMIT_BYTES = 56 * 1024 * 1024

N_MIXERS = 3
SB_HEADS = 8
SB_HEAD_DIM = 128


def _params(*sem):
    return pltpu.CompilerParams(dimension_semantics=sem, vmem_limit_bytes=V7X_VMEM_LIMIT_BYTES)


def _dot(a, b):
    return jnp.dot(a, b, preferred_element_type=F32)


def _dot_nt(a, b):
    return lax.dot_general(a, b, (((1,), (1,)), ((), ())), preferred_element_type=F32)


def _rms(x, g):
    return x * lax.rsqrt(jnp.mean(x * x, axis=-1, keepdims=True) + NORM_EPS) * g


def _resident(shape):
    return pl.BlockSpec(shape, lambda *_: (0,) * len(shape), pipeline_mode=pl.Buffered(1))


def _ffn_kernel(x_ref, g_ref, wg_ref, wu_ref, wd_ref, o_ref, acc_ref, *, fc):
    x = x_ref[...]
    xn = _rms(x, g_ref[...]).astype(BF16)
    d_ff = wg_ref.shape[1]
    for c in range(d_ff // fc):
        sl = slice(c * fc, (c + 1) * fc)
        gate = _dot(xn, wg_ref[:, sl])
        up = _dot(xn, wu_ref[:, sl])
        h = (gate / (1.0 + jnp.exp(-gate)) * up).astype(BF16)
        y = _dot(h, wd_ref[sl, :])
        if c == 0:
            acc_ref[...] = y
        else:
            acc_ref[...] += y
    o_ref[...] = x + 0.5 * acc_ref[...]


def _ffn(x, g, wg, wu, wd, *, tm=1024, fc=256):
    n, d = x.shape
    d_ff = wg.shape[1]
    return pl.pallas_call(
        functools.partial(_ffn_kernel, fc=fc),
        out_shape=jax.ShapeDtypeStruct((n, d), F32),
        grid=(n // tm,),
        in_specs=[
            pl.BlockSpec((tm, d), lambda i: (i, 0)),
            _resident((1, d)),
            _resident((d, d_ff)),
            _resident((d, d_ff)),
            _resident((d_ff, d)),
        ],
        out_specs=pl.BlockSpec((tm, d), lambda i: (i, 0)),
        scratch_shapes=[pltpu.VMEM((tm, d), F32)],
        compiler_params=_params("parallel"),
        name="ffn",
    )(x, g, wg, wu, wd)


def _norm_proj_kernel(x_ref, g_ref, w_ref, o_ref):
    xn = _rms(x_ref[...], g_ref[...]).astype(BF16)
    o_ref[...] = _dot(xn, w_ref[...]).astype(o_ref.dtype)


def _norm_proj(x, g, w, out_dtype, *, tm=512):
    n, d = x.shape
    m = w.shape[1]
    return pl.pallas_call(
        _norm_proj_kernel,
        out_shape=jax.ShapeDtypeStruct((n, m), out_dtype),
        grid=(n // tm,),
        in_specs=[pl.BlockSpec((tm, d), lambda i: (i, 0)), _resident((1, d)), _resident((d, m))],
        out_specs=pl.BlockSpec((tm, m), lambda i: (i, 0)),
        compiler_params=_params("parallel"),
        name="norm_proj",
    )(x, g, w)


def _out_proj_kernel(o_ref, w_ref, x_ref, y_ref):
    y_ref[...] = x_ref[...] + _dot(o_ref[...], w_ref[...])


def _out_proj(o, w, x, *, tm=1024):
    n, d = x.shape
    k = o.shape[1]
    return pl.pallas_call(
        _out_proj_kernel,
        out_shape=jax.ShapeDtypeStruct((n, d), F32),
        grid=(n // tm,),
        in_specs=[pl.BlockSpec((tm, k), lambda i: (i, 0)), _resident((k, d)), pl.BlockSpec((tm, d), lambda i: (i, 0))],
        out_specs=pl.BlockSpec((tm, d), lambda i: (i, 0)),
        compiler_params=_params("parallel"),
        name="out_proj",
    )(o, w, x)


def _final_norm_kernel(x_ref, g_ref, o_ref):
    o_ref[...] = _rms(x_ref[...], g_ref[...])


def _final_norm(x, g, *, tm=1024):
    n, d = x.shape
    return pl.pallas_call(
        _final_norm_kernel,
        out_shape=jax.ShapeDtypeStruct((n, d), F32),
        grid=(n // tm,),
        in_specs=[pl.BlockSpec((tm, d), lambda i: (i, 0)), _resident((1, d))],
        out_specs=pl.BlockSpec((tm, d), lambda i: (i, 0)),
        compiler_params=_params("parallel"),
        name="final_norm",
    )(x, g)


def _sb_kernel(q_ref, k_ref, v_ref, o_ref, *, t, scale):
    i = pl.program_id(2)
    q = q_ref[...]
    row = lax.broadcasted_iota(jnp.int32, (t, t), 0)
    col = lax.broadcasted_iota(jnp.int32, (t, t), 1)
    tri = jnp.where(row > col, 1.0, 0.0).astype(BF16)
    earlier = col < row

    def tile(j, carry, acc, diag):
        kj = k_ref[pl.ds(pl.multiple_of(j * t, t), t), :]
        vj = v_ref[pl.ds(pl.multiple_of(j * t, t), t), :]
        z = _dot_nt(q, kj) * scale
        log_beta = jnp.minimum(z, 0.0) - jnp.log1p(jnp.exp(-jnp.abs(z)))
        lf = log_beta - z
        if diag:
            lf = jnp.where(earlier, lf, 0.0)
        hi = lf.astype(BF16)
        lo = (lf - hi.astype(F32)).astype(BF16)
        rex = _dot(hi, tri) + _dot(lo, tri)
        att = jnp.exp(log_beta + rex + carry)
        if diag:
            att = jnp.where(earlier, att, 0.0)
        acc = acc + _dot(att.astype(BF16), vj)
        carry = carry + rex[:, 0:1] + lf[:, 0:1]
        return carry, acc

    carry0 = jnp.zeros((t, 1), F32)
    acc0 = jnp.zeros((t, q.shape[1]), F32)
    carry, acc = tile(i, carry0, acc0, True)

    def body(s, c):
        return tile(i - 1 - s, c[0], c[1], False)

    carry, acc = lax.fori_loop(0, i, body, (carry, acc))
    o_ref[...] = acc.astype(o_ref.dtype)


def _sb_attention(qkv, *, t=256):
    b, s, _ = qkv.shape
    h, d = SB_HEADS, SB_HEAD_DIM
    return pl.pallas_call(
        functools.partial(_sb_kernel, t=t, scale=d ** -0.5),
        out_shape=jax.ShapeDtypeStruct((b, s, h * d), BF16),
        grid=(b, h, s // t),
        in_specs=[
            pl.BlockSpec((None, t, d), lambda bi, hi, i: (bi, i, hi)),
            pl.BlockSpec((None, s, d), lambda bi, hi, i: (bi, 0, h + hi)),
            pl.BlockSpec((None, s, d), lambda bi, hi, i: (bi, 0, 2 * h + hi)),
        ],
        out_specs=pl.BlockSpec((None, t, d), lambda bi, hi, i: (bi, i, hi)),
        compiler_params=_params("parallel", "parallel", "arbitrary"),
        name="sb_attention",
    )(qkv, qkv, qkv)


def _stick_breaking_mixer(x, norm_g, w_in, w_out, bsz, seq):
    qkv = _norm_proj(x, norm_g, w_in.astype(BF16), BF16)
    o = _sb_attention(qkv.reshape(bsz, seq, -1))
    return _out_proj(o.reshape(bsz * seq, -1), w_out.astype(BF16), x)


GDN_HEADS = 8
GDN_HEAD_DIM = 128
GDN_DIM = GDN_HEADS * GDN_HEAD_DIM
GDN_CHUNK = 64
GDN_BLOCK = 128
GDN_CONV = 4
CONV_TAIL_ROWS = 8


def _split2(a):
    hi = a.astype(BF16)
    return hi, (a - hi.astype(F32)).astype(BF16)


def _dot3(a, b):
    ah, al = _split2(a)
    bh, bl = _split2(b)
    return _dot(ah, bh) + (_dot(ah, bl) + _dot(al, bh))


def _dot_exact_rhs(a, b01):
    h0 = a.astype(BF16)
    r1 = a - h0.astype(F32)
    h1 = r1.astype(BF16)
    h2 = (r1 - h1.astype(F32)).astype(BF16)
    return _dot(b01, h0) + (_dot(b01, h1) + _dot(b01, h2))


def _sigmoid(x):
    return 1.0 / (1.0 + jnp.exp(-x))


def _softplus(x):
    return jnp.maximum(x, 0.0) + jnp.log1p(jnp.exp(-jnp.abs(x)))


def _gdn_kernel(main_ref, small_ref, conv_ref, alog_ref, dt_ref, ng_ref, o_ref, state_ref, tail_ref):
    t = GDN_BLOCK
    c = GDN_CHUNK
    d = GDN_HEAD_DIM
    nqkv = 3 * GDN_DIM

    @pl.when(pl.program_id(1) == 0)
    def _():
        state_ref[...] = jnp.zeros_like(state_ref)
        tail_ref[...] = jnp.zeros_like(tail_ref)

    cur = main_ref[:, :nqkv]
    xe = jnp.concatenate([tail_ref[...], cur], axis=0)
    conv = cur * conv_ref[GDN_CONV - 1:GDN_CONV, :]
    for k in range(1, GDN_CONV):
        conv = conv + pltpu.roll(xe, k, axis=0)[CONV_TAIL_ROWS:, :] * conv_ref[GDN_CONV - 1 - k:GDN_CONV - k, :]
    tail_ref[...] = cur[t - CONV_TAIL_ROWS:, :]
    qkv = conv * _sigmoid(conv)

    small = small_ref[...]
    beta_all = _sigmoid(small)
    g_all = -jnp.exp(alog_ref[...]) * _softplus(small + dt_ref[...])
    r2 = lax.broadcasted_iota(jnp.int32, (t, t), 0)
    c2 = lax.broadcasted_iota(jnp.int32, (t, t), 1)
    same_chunk = (r2 // c) == (c2 // c)
    tril_bd = jnp.where(same_chunk & (c2 <= r2), 1.0, 0.0).astype(BF16)
    gc_all = _dot_exact_rhs(g_all, tril_bd)
    gc_t = gc_all.T

    ri = lax.broadcasted_iota(jnp.int32, (c, c), 0)
    ci = lax.broadcasted_iota(jnp.int32, (c, c), 1)
    incl = ci <= ri
    strict = ci < ri
    eye = jnp.where(ri == ci, 1.0, 0.0)
    ng = ng_ref[...]

    for ch in range(t // c):
        rows = slice(ch * c, (ch + 1) * c)
        for h in range(GDN_HEADS):
            cols = slice(h * d, (h + 1) * d)
            q = qkv[rows, h * d:(h + 1) * d]
            k = qkv[rows, GDN_DIM + h * d:GDN_DIM + (h + 1) * d]
            v = qkv[rows, 2 * GDN_DIM + h * d:2 * GDN_DIM + (h + 1) * d]
            z = main_ref[rows, nqkv + h * d:nqkv + (h + 1) * d]
            q = q * lax.rsqrt(jnp.sum(q * q, axis=-1, keepdims=True) + NORM_EPS) * (d ** -0.5)
            k = k * lax.rsqrt(jnp.sum(k * k, axis=-1, keepdims=True) + NORM_EPS)
            beta = beta_all[rows, h:h + 1]
            gc_col = gc_all[rows, GDN_HEADS + h:GDN_HEADS + h + 1]
            gc_row = gc_t[GDN_HEADS + h:GDN_HEADS + h + 1, rows]
            gc_last = gc_col[c - 1:c, :]
            decay = jnp.where(incl, jnp.exp(jnp.where(incl, gc_col - gc_row, 0.0)), 0.0)
            kb = k.astype(BF16)
            kk = _dot_nt(kb, kb)
            neg_l = jnp.where(strict, -(beta * kk * decay), 0.0)
            inv = eye + neg_l
            pw = neg_l
            for _ in range(int(math.log2(c)) - 1):
                pw = _dot3(pw, pw)
                inv = inv + _dot3(inv, pw)
            e_gc = jnp.exp(gc_col)
            rhs = jnp.concatenate([v * beta, k * (beta * e_gc)], axis=-1)
            sol = _dot3(inv, rhs)
            u_in, w = sol[:, :d], sol[:, d:]
            qk = _dot_nt(q.astype(BF16), kb) * decay
            q_dec = q * e_gc
            k_dec = k * jnp.exp(gc_last - gc_col)
            g_tot = jnp.exp(gc_last)
            state = state_ref[h]
            sb = state.astype(BF16)
            u = u_in - _dot(w.astype(BF16), sb)
            ub = u.astype(BF16)
            o = _dot(q_dec.astype(BF16), sb) + _dot(qk.astype(BF16), ub)
            state_ref[h] = state * g_tot + lax.dot_general(
                k_dec.astype(BF16), ub, (((0,), (0,)), ((), ())), preferred_element_type=F32)
            o = o * lax.rsqrt(jnp.mean(o * o, axis=-1, keepdims=True) + NORM_EPS) * ng
            o_ref[rows, cols] = (o * (z * _sigmoid(z))).astype(o_ref.dtype)


def _gdn_core(proj, conv_w, alog_pad, dt_pad, norm_g):
    b, s, _ = proj.shape
    t = GDN_BLOCK
    return pl.pallas_call(
        _gdn_kernel,
        out_shape=jax.ShapeDtypeStruct((b, s, GDN_DIM), BF16),
        grid=(b, s // t),
        in_specs=[
            pl.BlockSpec((None, t, 4 * GDN_DIM), lambda bi, i: (bi, i, 0)),
            pl.BlockSpec((None, t, 128), lambda bi, i: (bi, i, 4 * GDN_DIM // 128)),
            _resident((GDN_CONV, 3 * GDN_DIM)),
            _resident((1, 128)),
            _resident((1, 128)),
            _resident((1, GDN_HEAD_DIM)),
        ],
        out_specs=pl.BlockSpec((None, t, GDN_DIM), lambda bi, i: (bi, i, 0)),
        scratch_shapes=[
            pltpu.VMEM((GDN_HEADS, GDN_HEAD_DIM, GDN_HEAD_DIM), F32),
            pltpu.VMEM((CONV_TAIL_ROWS, 3 * GDN_DIM), F32),
        ],
        compiler_params=_params("parallel", "arbitrary"),
        name="gdn_core",
    )(proj, proj, conv_w, alog_pad, dt_pad, norm_g)


def _gdn_mixer(x, norm_g, w_in, conv_w, a_log, dt_bias, head_norm, w_out, bsz, seq):
    pad = 128 - 2 * GDN_HEADS
    w = jnp.pad(w_in, ((0, 0), (0, pad))).astype(BF16)
    proj = _norm_proj(x, norm_g, w, F32)
    zeros = jnp.zeros((GDN_HEADS,), F32)
    alog_pad = jnp.pad(jnp.concatenate([zeros, a_log]), (0, pad))[None]
    dt_pad = jnp.pad(jnp.concatenate([zeros, dt_bias]), (0, pad))[None]
    o = _gdn_core(proj.reshape(bsz, seq, -1), conv_w, alog_pad, dt_pad, head_norm[None])
    return _out_proj(o.reshape(bsz * seq, -1), w_out.astype(BF16), x)


DSA_HEADS = 8
DSA_Q_RANK = 384
DSA_KV_RANK = 256
DSA_ROPE_DIM = 32
DSA_NOPE_DIM = 96
DSA_QK_DIM = DSA_ROPE_DIM + DSA_NOPE_DIM
DSA_V_DIM = 128
DSA_IDX_HEADS = 8
DSA_IDX_DIM = 64
DSA_IDX_ROPE = DSA_IDX_DIM // 4
DSA_TOPK_MAX = 256
DSA_LAT = DSA_KV_RANK + DSA_ROPE_DIM
LANES = 128
DSA_TQ = 128
DSA_TK = 256
MASKED = -1e30
INT32_MIN = -2 ** 31

_IN_CQ = 0
_IN_CKV = 384
_IN_KROPE = 640
_IN_KROPE_ROT = 768
_IN_KIDX = 896
_IN_KIDX_ROT = 1024
_IN_WIDX = 1152
_IN_COLS = 1280
_TAB_COS = 0
_TAB_SIN = DSA_HEADS * DSA_ROPE_DIM
_TAB_ICOS = 2 * DSA_HEADS * DSA_ROPE_DIM
_TAB_ISIN = _TAB_ICOS + DSA_IDX_HEADS * DSA_IDX_DIM
_TAB_COLS = _TAB_ISIN + DSA_IDX_HEADS * DSA_IDX_DIM


def _dsa_prep_kernel(x_ref, mg_ref, win_ref, cqg_ref, ckvg_ref, kig_ref, kigr_ref, wnope_ref, wrope_ref,
                     wroper_ref, wqi_ref, wqir_ref, wukt_ref, tab_ref,
                     qidx_ref, widx_ref, qlat_ref, kidx_ref, klat_ref):
    h = _rms(x_ref[...], mg_ref[...]).astype(BF16)
    proj = _dot(h, win_ref[...])
    cq = _rms(proj[:, _IN_CQ:_IN_CQ + DSA_Q_RANK], cqg_ref[...]).astype(BF16)
    ckv = _rms(proj[:, _IN_CKV:_IN_CKV + DSA_KV_RANK], ckvg_ref[...])
    tab = tab_ref[...]
    r, di = DSA_ROPE_DIM, DSA_IDX_DIM
    k_rope = (proj[:, _IN_KROPE:_IN_KROPE + r] * tab[:, _TAB_COS:_TAB_COS + r]
              + proj[:, _IN_KROPE_ROT:_IN_KROPE_ROT + r] * tab[:, _TAB_SIN:_TAB_SIN + r])
    klat_ref[:, :DSA_KV_RANK] = ckv.astype(BF16)
    klat_ref[:, DSA_KV_RANK:] = k_rope.astype(BF16)
    kraw = proj[:, _IN_KIDX:_IN_KIDX + di]
    inv = lax.rsqrt(jnp.mean(kraw * kraw, axis=-1, keepdims=True) + NORM_EPS)
    ki = kraw * inv * kig_ref[...]
    kir = proj[:, _IN_KIDX_ROT:_IN_KIDX_ROT + di] * inv * kigr_ref[...]
    kidx_ref[...] = (ki * tab[:, _TAB_ICOS:_TAB_ICOS + di] + kir * tab[:, _TAB_ISIN:_TAB_ISIN + di]).astype(BF16)
    widx_ref[...] = proj[:, _IN_WIDX:_IN_WIDX + LANES] * (DSA_IDX_HEADS ** -0.5 * DSA_IDX_DIM ** -0.5)
    qi = (_dot(cq, wqi_ref[...]) * tab[:, _TAB_ICOS:_TAB_ISIN]
          + _dot(cq, wqir_ref[...]) * tab[:, _TAB_ISIN:_TAB_COLS])
    qidx_ref[...] = qi.astype(BF16)
    q_rope = (_dot(cq, wrope_ref[...]) * tab[:, _TAB_COS:_TAB_SIN]
              + _dot(cq, wroper_ref[...]) * tab[:, _TAB_SIN:_TAB_ICOS])
    q_nope = _dot(cq, wnope_ref[...]).astype(BF16)
    for hh in range(DSA_HEADS):
        q_abs = _dot(q_nope[:, hh * LANES:(hh + 1) * LANES], wukt_ref[hh])
        qlat_ref[hh, :, :DSA_KV_RANK] = q_abs.astype(BF16)
        qlat_ref[hh, :, DSA_KV_RANK:] = q_rope[:, hh * r:(hh + 1) * r].astype(BF16)


def _rot_half(w):
    half = w.shape[-1] // 2
    return jnp.concatenate([-w[..., half:], w[..., :half]], axis=-1)


def _rope_tables(seq):
    pos = jnp.arange(seq, dtype=F32)[:, None]

    def cos_sin(r):
        half = r // 2
        inv_freq = ROPE_THETA ** (-jnp.arange(half, dtype=F32) * (2.0 / r))
        ang = pos * inv_freq[None, :]
        return jnp.cos(ang), jnp.sin(ang)

    c, s = cos_sin(DSA_ROPE_DIM)
    ic, isn = cos_sin(DSA_IDX_ROPE)
    rest = DSA_IDX_DIM - DSA_IDX_ROPE
    icos = jnp.concatenate([ic, ic, jnp.ones((seq, rest), F32)], axis=-1)
    isin = jnp.concatenate([isn, isn, jnp.zeros((seq, rest), F32)], axis=-1)
    return jnp.concatenate([jnp.tile(jnp.concatenate([c, c], -1), (1, DSA_HEADS)),
                            jnp.tile(jnp.concatenate([s, s], -1), (1, DSA_HEADS)),
                            jnp.tile(icos, (1, DSA_IDX_HEADS)), jnp.tile(isin, (1, DSA_IDX_HEADS))], axis=-1)


def _dsa_prep(x, norm_g, w_in, cq_g, ckv_g, kidx_g, w_uq, w_qidx, w_uk, bsz, seq, *, tm=256):
    n, d = x.shape
    o1 = DSA_Q_RANK
    o2 = o1 + DSA_KV_RANK
    o3 = o2 + DSA_ROPE_DIM
    o4 = o3 + DSA_IDX_DIM
    ri = DSA_IDX_ROPE

    def padc(w, cols):
        return jnp.pad(w, ((0, 0), (0, cols - w.shape[1])))

    w_kidx = w_in[:, o3:o4]
    w_ext = jnp.concatenate([
        w_in[:, :o2],
        padc(w_in[:, o2:o3], LANES), padc(_rot_half(w_in[:, o2:o3]), LANES),
        padc(w_kidx, LANES), padc(_rot_half(w_kidx[:, :ri]), LANES),
        padc(w_in[:, o4:], LANES)], axis=1).astype(BF16)
    kig_rot = jnp.pad(jnp.concatenate([kidx_g[ri // 2:ri], kidx_g[:ri // 2]]), (0, DSA_IDX_DIM - ri))
    uq = w_uq.reshape(DSA_Q_RANK, DSA_HEADS, DSA_QK_DIM)
    w_rope = uq[:, :, :DSA_ROPE_DIM]
    w_nope = jnp.pad(uq[:, :, DSA_ROPE_DIM:], ((0, 0), (0, 0), (0, LANES - DSA_NOPE_DIM)))
    qi = w_qidx.reshape(DSA_Q_RANK, DSA_IDX_HEADS, DSA_IDX_DIM)
    qi_rot = jnp.pad(_rot_half(qi[:, :, :ri]), ((0, 0), (0, 0), (0, DSA_IDX_DIM - ri)))
    wukt = jnp.transpose(w_uk.reshape(DSA_KV_RANK, DSA_HEADS, DSA_NOPE_DIM), (1, 2, 0))
    wukt = jnp.pad(wukt, ((0, 0), (0, LANES - DSA_NOPE_DIM), (0, 0))).astype(BF16)
    tab = _rope_tables(seq)
    nblk = seq // tm
    flat = lambda a: a.reshape(DSA_Q_RANK, -1).astype(BF16)
    tok = lambda c: pl.BlockSpec((tm, c), lambda i: (i, 0))
    return pl.pallas_call(
        _dsa_prep_kernel,
        out_shape=(
            jax.ShapeDtypeStruct((n, DSA_IDX_HEADS * DSA_IDX_DIM), BF16),
            jax.ShapeDtypeStruct((n, LANES), F32),
            jax.ShapeDtypeStruct((DSA_HEADS, n, DSA_LAT), BF16),
            jax.ShapeDtypeStruct((n, DSA_IDX_DIM), BF16),
            jax.ShapeDtypeStruct((n, DSA_LAT), BF16),
        ),
        grid=(n // tm,),
        in_specs=[
            tok(d), _resident((1, d)), _resident((d, _IN_COLS)),
            _resident((1, DSA_Q_RANK)), _resident((1, DSA_KV_RANK)),
            _resident((1, DSA_IDX_DIM)), _resident((1, DSA_IDX_DIM)),
            _resident((DSA_Q_RANK, DSA_HEADS * LANES)), _resident((DSA_Q_RANK, DSA_HEADS * DSA_ROPE_DIM)),
            _resident((DSA_Q_RANK, DSA_HEADS * DSA_ROPE_DIM)),
            _resident((DSA_Q_RANK, DSA_IDX_HEADS * DSA_IDX_DIM)), _resident((DSA_Q_RANK, DSA_IDX_HEADS * DSA_IDX_DIM)),
            _resident((DSA_HEADS, LANES, DSA_KV_RANK)),
            pl.BlockSpec((tm, _TAB_COLS), lambda i: (i % nblk, 0)),
        ],
        out_specs=(
            tok(DSA_IDX_HEADS * DSA_IDX_DIM), tok(LANES),
            pl.BlockSpec((DSA_HEADS, tm, DSA_LAT), lambda i: (0, i, 0)),
            tok(DSA_IDX_DIM), tok(DSA_LAT),
        ),
        compiler_params=_params("parallel"),
        name="dsa_prep",
    )(x, norm_g, w_ext, cq_g[None], ckv_g[None], kidx_g[None], kig_rot[None],
      flat(w_nope), flat(w_rope), flat(_rot_half(w_rope)), flat(qi), flat(qi_rot), wukt, tab)


def _dsa_attn_kernel(qidx_ref, widx_ref, qlat_ref, kidx_ref, klat_ref, o_ref,
                     key_ref, bias_ref, m_ref, l_ref, acc_ref, *, topk, scale):
    i = pl.program_id(1)
    tq, tk = DSA_TQ, DSA_TK
    sub = tk // LANES
    nsteps = (i * tq + tq + tk - 1) // tk
    w_t = widx_ref[...].T
    qpos = i * tq + lax.broadcasted_iota(jnp.int32, (1, tq), 1)
    krow = lax.broadcasted_iota(jnp.int32, (LANES, tq), 0)

    def score_step(jj, _):
        for u in range(sub):
            j = jj * sub + u
            kj = kidx_ref[pl.ds(pl.multiple_of(j * LANES, LANES), LANES), :]
            s = jnp.zeros((LANES, tq), F32)
            for h in range(DSA_IDX_HEADS):
                lg = _dot_nt(kj, qidx_ref[:, h * DSA_IDX_DIM:(h + 1) * DSA_IDX_DIM])
                s = s + w_t[h:h + 1, :] * jnp.maximum(lg, 0.0)
            s = jnp.where(s == 0.0, 0.0, s)
            bits = lax.bitcast_convert_type(s, jnp.int32)
            key = bits ^ ((bits >> 31) & 0x7FFFFFFF)
            key = jnp.where(j * LANES + krow <= qpos, key, INT32_MIN)
            key_ref[pl.ds(pl.multiple_of(j * LANES, LANES), LANES), :] = key
        return 0

    lax.fori_loop(0, nsteps, score_step, 0)

    def count_ge(trial):
        def body(jj, acc):
            blk = key_ref[pl.ds(pl.multiple_of(jj * tk, tk), tk), :]
            hit = jnp.where(blk >= trial, 1.0, 0.0)
            return acc + jnp.sum(hit.reshape(tk // 8, 8, tq), axis=0)
        acc = lax.fori_loop(0, nsteps, body, jnp.zeros((8, tq), F32))
        return jnp.sum(acc, axis=0, keepdims=True)

    def bit_step(it, r):
        trial = r + jnp.left_shift(jnp.int32(1), 31 - it)
        return jnp.where(count_ge(trial) >= topk, trial, r)

    kth = lax.fori_loop(0, 32, bit_step, jnp.full((1, tq), INT32_MIN, jnp.int32))
    n_ge = count_ge(kth)
    n_gt = count_ge(kth + 1)
    need = topk - n_gt
    valid_min = INT32_MIN + 1
    has_tie = jnp.where((n_ge > topk) & (kth >= valid_min), 1.0, 0.0)
    tie_any = jnp.max(has_tie) > 0.0

    def write_bias(jj, u, sel):
        bias_ref[jj, :, u * LANES:(u + 1) * LANES] = jnp.where(sel, 0.0, MASKED).T

    @pl.when(jnp.logical_not(tie_any))
    def _():
        thr = jnp.maximum(kth, valid_min)

        def body(jj, _):
            for u in range(sub):
                blk = key_ref[pl.ds(pl.multiple_of((jj * sub + u) * LANES, LANES), LANES), :]
                write_bias(jj, u, blk >= thr)
            return 0

        lax.fori_loop(0, nsteps, body, 0)

    @pl.when(tie_any)
    def _():
        r2 = lax.broadcasted_iota(jnp.int32, (LANES, LANES), 0)
        c2 = lax.broadcasted_iota(jnp.int32, (LANES, LANES), 1)
        before = jnp.where(c2 < r2, 1.0, 0.0).astype(BF16)

        def body(jj, run):
            for u in range(sub):
                blk = key_ref[pl.ds(pl.multiple_of((jj * sub + u) * LANES, LANES), LANES), :]
                eq = jnp.where((blk == kth) & (blk >= valid_min), 1.0, 0.0)
                rank = _dot(before, eq.astype(BF16)) + run
                take = jnp.where(rank < need, eq, 0.0)
                write_bias(jj, u, jnp.where(blk > kth, 1.0, take) > 0.0)
                run = run + jnp.sum(eq, axis=0, keepdims=True)
            return run

        lax.fori_loop(0, nsteps, body, jnp.zeros((1, tq), F32))

    q = qlat_ref[...].reshape(DSA_HEADS * tq, DSA_LAT)
    m_ref[...] = jnp.full_like(m_ref, MASKED)
    l_ref[...] = jnp.zeros_like(l_ref)
    acc_ref[...] = jnp.zeros_like(acc_ref)

    def attn_step(jj, _):
        kt = klat_ref[pl.ds(pl.multiple_of(jj * tk, tk), tk), :]
        att = _dot_nt(q, kt) * scale
        att = (att.reshape(DSA_HEADS, tq, tk) + bias_ref[jj][None]).reshape(DSA_HEADS * tq, tk)
        m_old = m_ref[...]
        m_new = jnp.maximum(m_old, jnp.max(att, axis=-1, keepdims=True))
        alpha = jnp.exp(m_old - m_new)
        p = jnp.exp(att - m_new)
        l_ref[...] = alpha * l_ref[...] + jnp.sum(p, axis=-1, keepdims=True)
        acc_ref[...] = alpha * acc_ref[...] + _dot(p.astype(BF16), kt[:, :DSA_KV_RANK])
        m_ref[...] = m_new
        return 0

    lax.fori_loop(0, nsteps, attn_step, 0)
    o = acc_ref[...] / l_ref[...]
    o_ref[...] = o.reshape(DSA_HEADS, tq, DSA_KV_RANK).astype(o_ref.dtype)


def _dsa_attn(qidx, widx, qlat, kidx, klat, topk):
    b, s, _ = qidx.shape
    tq = DSA_TQ
    return pl.pallas_call(
        functools.partial(_dsa_attn_kernel, topk=topk, scale=DSA_QK_DIM ** -0.5),
        out_shape=jax.ShapeDtypeStruct((DSA_HEADS, b, s, DSA_KV_RANK), BF16),
        grid=(b, s // tq),
        in_specs=[
            pl.BlockSpec((None, tq, DSA_IDX_HEADS * DSA_IDX_DIM), lambda bi, i: (bi, i, 0)),
            pl.BlockSpec((None, tq, LANES), lambda bi, i: (bi, i, 0)),
            pl.BlockSpec((DSA_HEADS, None, tq, DSA_LAT), lambda bi, i: (0, bi, i, 0)),
            pl.BlockSpec((None, s, DSA_IDX_DIM), lambda bi, i: (bi, 0, 0)),
            pl.BlockSpec((None, s, DSA_LAT), lambda bi, i: (bi, 0, 0)),
        ],
        out_specs=pl.BlockSpec((DSA_HEADS, None, tq, DSA_KV_RANK), lambda bi, i: (0, bi, i, 0)),
        scratch_shapes=[
            pltpu.VMEM((s, tq), jnp.int32),
            pltpu.VMEM((s // DSA_TK, tq, DSA_TK), F32),
            pltpu.VMEM((DSA_HEADS * tq, 1), F32),
            pltpu.VMEM((DSA_HEADS * tq, 1), F32),
            pltpu.VMEM((DSA_HEADS * tq, DSA_KV_RANK), F32),
        ],
        compiler_params=_params("parallel", "arbitrary"),
        name="dsa_attn",
    )(qidx, widx, qlat, kidx, klat)


def _dsa_out_kernel(olat_ref, wuv_ref, wout_ref, x_ref, y_ref):
    parts = [_dot(olat_ref[h], wuv_ref[h]) for h in range(DSA_HEADS)]
    o = jnp.concatenate(parts, axis=-1).astype(BF16)
    y_ref[...] = x_ref[...] + _dot(o, wout_ref[...])


def _dsa_out(olat, w_uv, w_out, x, *, tm=512):
    n, d = x.shape
    wuv = jnp.transpose(w_uv.reshape(DSA_KV_RANK, DSA_HEADS, DSA_V_DIM), (1, 0, 2)).astype(BF16)
    return pl.pallas_call(
        _dsa_out_kernel,
        out_shape=jax.ShapeDtypeStruct((n, d), F32),
        grid=(n // tm,),
        in_specs=[
            pl.BlockSpec((DSA_HEADS, tm, DSA_KV_RANK), lambda i: (0, i, 0)),
            _resident((DSA_HEADS, DSA_KV_RANK, DSA_V_DIM)),
            _resident((DSA_HEADS * DSA_V_DIM, d)),
            pl.BlockSpec((tm, d), lambda i: (i, 0)),
        ],
        out_specs=pl.BlockSpec((tm, d), lambda i: (i, 0)),
        compiler_params=_params("parallel"),
        name="dsa_out",
    )(olat, wuv, w_out.astype(BF16), x)


def _dsa_mixer(x, norm_g, w_in, cq_g, ckv_g, kidx_g, w_uq, w_qidx, w_uk, w_uv, w_out, bsz, seq):
    qidx, widx, qlat, kidx, klat = _dsa_prep(x, norm_g, w_in, cq_g, ckv_g, kidx_g, w_uq, w_qidx, w_uk, bsz, seq)
    topk = min(DSA_TOPK_MAX, seq // 4)
    b3 = lambda a: a.reshape(bsz, seq, -1)
    olat = _dsa_attn(b3(qidx), b3(widx), qlat.reshape(DSA_HEADS, bsz, seq, DSA_LAT), b3(kidx), b3(klat), topk)
    return _dsa_out(olat.reshape(DSA_HEADS, bsz * seq, DSA_KV_RANK), w_uv, w_out, x)


def kernel(x, ffn1_norm, ffn1_w_gu, ffn1_w_down, mix_norm, ffn2_norm, ffn2_w_gu, ffn2_w_down, gdn_w_in, gdn_conv, gdn_a_log, gdn_dt_bias, gdn_norm, gdn_w_out, sb_w_in, sb_w_out, dsa_w_in, dsa_cq_norm, dsa_ckv_norm, dsa_kidx_norm, dsa_w_uq, dsa_w_qidx, dsa_w_uk, dsa_w_uv, dsa_w_out, final_norm):
    bsz, seq, d = x.shape
    d_ff = ffn1_w_down.shape[1]
    xf = x.reshape(bsz * seq, d)

    def ffn(xf, g, w_gu, w_down):
        return _ffn(xf, g[None], w_gu[:, :d_ff].astype(BF16), w_gu[:, d_ff:].astype(BF16), w_down.astype(BF16))

    for i in range(ffn1_norm.shape[0]):
        kind, j = i % N_MIXERS, i // N_MIXERS
        xf = ffn(xf, ffn1_norm[i], ffn1_w_gu[i], ffn1_w_down[i])
        g = mix_norm[i][None]
        if kind == 0:
            xf = _gdn_mixer(xf, g, gdn_w_in[j], gdn_conv[j], gdn_a_log[j], gdn_dt_bias[j], gdn_norm[j],
                            gdn_w_out[j], bsz, seq)
        elif kind == 1:
            xf = _stick_breaking_mixer(xf, g, sb_w_in[j], sb_w_out[j], bsz, seq)
        else:
            xf = _dsa_mixer(xf, g, dsa_w_in[j], dsa_cq_norm[j], dsa_ckv_norm[j], dsa_kidx_norm[j], dsa_w_uq[j],
                            dsa_w_qidx[j], dsa_w_uk[j], dsa_w_uv[j], dsa_w_out[j], bsz, seq)
        xf = ffn(xf, ffn2_norm[i], ffn2_w_gu[i], ffn2_w_down[i])
    return _final_norm(xf, final_norm[None]).reshape(bsz, seq, d)
```

```python
import functools
import math

import jax
import jax.numpy as jnp
from jax import lax
from jax.experimental import pallas as pl
from jax.experimental.pallas import tpu as pltpu

F32 = jnp.float32
BF16 = jnp.bfloat16

NORM_EPS = 1e-6
ROPE_THETA = 500000.0
V7X_VMEM_LIMIT_BYTES = 56 * 1024 * 1024

N_MIXERS = 3
SB_HEADS = 8
SB_HEAD_DIM = 128


def _params(*sem):
    return pltpu.CompilerParams(dimension_semantics=sem, vmem_limit_bytes=V7X_VMEM_LIMIT_BYTES)


def _dot(a, b):
    return jnp.dot(a, b, preferred_element_type=F32)


def _dot_nt(a, b):
    return lax.dot_general(a, b, (((1,), (1,)), ((), ())), preferred_element_type=F32)


def _rms(x, g):
    return x * lax.rsqrt(jnp.mean(x * x, axis=-1, keepdims=True) + NORM_EPS) * g


def _resident(shape):
    return pl.BlockSpec(shape, lambda *_: (0,) * len(shape), pipeline_mode=pl.Buffered(1))


def _ffn_kernel(x_ref, g_ref, wg_ref, wu_ref, wd_ref, o_ref, acc_ref, *, fc):
    x = x_ref[...]
    xn = _rms(x, g_ref[...]).astype(BF16)
    d_ff = wg_ref.shape[1]
    for c in range(d_ff // fc):
        sl = slice(c * fc, (c + 1) * fc)
        gate = _dot(xn, wg_ref[:, sl])
        up = _dot(xn, wu_ref[:, sl])
        h = (gate / (1.0 + jnp.exp(-gate)) * up).astype(BF16)
        y = _dot(h, wd_ref[sl, :])
        if c == 0:
            acc_ref[...] = y
        else:
            acc_ref[...] += y
    o_ref[...] = x + 0.5 * acc_ref[...]


def _ffn(x, g, wg, wu, wd, *, tm=1024, fc=256):
    n, d = x.shape
    d_ff = wg.shape[1]
    return pl.pallas_call(
        functools.partial(_ffn_kernel, fc=fc),
        out_shape=jax.ShapeDtypeStruct((n, d), F32),
        grid=(n // tm,),
        in_specs=[
            pl.BlockSpec((tm, d), lambda i: (i, 0)),
            _resident((1, d)),
            _resident((d, d_ff)),
            _resident((d, d_ff)),
            _resident((d_ff, d)),
        ],
        out_specs=pl.BlockSpec((tm, d), lambda i: (i, 0)),
        scratch_shapes=[pltpu.VMEM((tm, d), F32)],
        compiler_params=_params("parallel"),
        name="ffn",
    )(x, g, wg, wu, wd)


def _norm_proj_kernel(x_ref, g_ref, w_ref, *rest):
    o_ref = rest[-1]
    xn = _rms(x_ref[...], g_ref[...]).astype(BF16)
    y = _dot(xn, w_ref[...])
    if len(rest) == 2:
        y = y * rest[0][...]
    o_ref[...] = y.astype(o_ref.dtype)


def _norm_proj(x, g, w, out_dtype, col_scale=None, *, tm=512):
    n, d = x.shape
    m = w.shape[1]
    extra = [] if col_scale is None else [col_scale]
    return pl.pallas_call(
        _norm_proj_kernel,
        out_shape=jax.ShapeDtypeStruct((n, m), out_dtype),
        grid=(n // tm,),
        in_specs=[pl.BlockSpec((tm, d), lambda i: (i, 0)), _resident((1, d)), _resident((d, m))]
        + [_resident((1, m))] * len(extra),
        out_specs=pl.BlockSpec((tm, m), lambda i: (i, 0)),
        compiler_params=_params("parallel"),
        name="norm_proj",
    )(x, g, w, *extra)


def _out_proj_kernel(o_ref, w_ref, x_ref, y_ref):
    y_ref[...] = x_ref[...] + _dot(o_ref[...], w_ref[...])


def _out_proj(o, w, x, *, tm=1024):
    n, d = x.shape
    k = o.shape[1]
    return pl.pallas_call(
        _out_proj_kernel,
        out_shape=jax.ShapeDtypeStruct((n, d), F32),
        grid=(n // tm,),
        in_specs=[pl.BlockSpec((tm, k), lambda i: (i, 0)), _resident((k, d)), pl.BlockSpec((tm, d), lambda i: (i, 0))],
        out_specs=pl.BlockSpec((tm, d), lambda i: (i, 0)),
        compiler_params=_params("parallel"),
        name="out_proj",
    )(o, w, x)


def _final_norm_kernel(x_ref, g_ref, o_ref):
    o_ref[...] = _rms(x_ref[...], g_ref[...])


def _final_norm(x, g, *, tm=1024):
    n, d = x.shape
    return pl.pallas_call(
        _final_norm_kernel,
        out_shape=jax.ShapeDtypeStruct((n, d), F32),
        grid=(n // tm,),
        in_specs=[pl.BlockSpec((tm, d), lambda i: (i, 0)), _resident((1, d))],
        out_specs=pl.BlockSpec((tm, d), lambda i: (i, 0)),
        compiler_params=_params("parallel"),
        name="final_norm",
    )(x, g)


SB_LINEAR_ABOVE = 30.0
SB_HEADS_PER_STEP = 8


def _sb_kernel(q_ref, k_ref, v_ref, o_ref, *, t):
    i = pl.program_id(2)
    d = SB_HEAD_DIM
    nh = SB_HEADS_PER_STEP
    row = lax.broadcasted_iota(jnp.int32, (t, t), 0)
    col = lax.broadcasted_iota(jnp.int32, (t, t), 1)
    tri = jnp.where(row >= col, 1.0, 0.0).astype(BF16)
    tri2 = jnp.concatenate([tri, tri], axis=0)
    earlier = col < row
    qs = [q_ref[:, h * d:(h + 1) * d] for h in range(nh)]

    def tiles(j, st, diag):
        ks = pl.ds(pl.multiple_of(j * t, t), t)
        hs = range(nh)
        zs = [_dot_nt(qs[h], k_ref[ks, h * d:(h + 1) * d]) for h in hs]
        sps = [jnp.where(z > SB_LINEAR_ABOVE, z, jnp.log2(1.0 + jnp.exp2(z))) for z in zs]
        if diag:
            sps = [jnp.where(earlier, sp, 0.0) for sp in sps]
        rins = [_dot(jnp.concatenate(_split2(sp), axis=-1), tri2) for sp in sps]
        atts = [jnp.exp2(zs[h] - rins[h] - st[h][0]) for h in hs]
        if diag:
            atts = [jnp.where(earlier, a, 0.0) for a in atts]
        accs = [st[h][1] + _dot(atts[h].astype(BF16), v_ref[ks, h * d:(h + 1) * d]) for h in hs]
        return tuple((st[h][0] + rins[h][:, 0:1], accs[h]) for h in hs)

    state = tiles(i, tuple((jnp.zeros((t, 1), F32), jnp.zeros((t, d), F32)) for _ in range(nh)), True)
    state = lax.fori_loop(0, i, lambda s, st: tiles(i - 1 - s, st, False), state)
    for h in range(nh):
        o_ref[:, h * d:(h + 1) * d] = state[h][1].astype(o_ref.dtype)


def _sb_attention(qkv, *, t=256):
    b, s, _ = qkv.shape
    h, d = SB_HEADS, SB_HEAD_DIM
    nh = SB_HEADS_PER_STEP
    w = nh * d
    ng = h // nh
    return pl.pallas_call(
        functools.partial(_sb_kernel, t=t),
        out_shape=jax.ShapeDtypeStruct((b, s, h * d), BF16),
        grid=(b, ng, s // t),
        in_specs=[
            pl.BlockSpec((None, t, w), lambda bi, gi, i: (bi, i, gi)),
            pl.BlockSpec((None, s, w), lambda bi, gi, i: (bi, 0, ng + gi)),
            pl.BlockSpec((None, s, w), lambda bi, gi, i: (bi, 0, 2 * ng + gi)),
        ],
        out_specs=pl.BlockSpec((None, t, w), lambda bi, gi, i: (bi, i, gi)),
        compiler_params=_params("parallel", "parallel", "arbitrary"),
        name="sb_attention",
    )(qkv, qkv, qkv)


def _stick_breaking_mixer(x, norm_g, w_in, w_out, bsz, seq):
    nq = SB_HEADS * SB_HEAD_DIM
    col_scale = jnp.concatenate([jnp.full((nq,), SB_HEAD_DIM ** -0.5 * math.log2(math.e), F32),
                                 jnp.ones((2 * nq,), F32)])[None]
    qkv = _norm_proj(x, norm_g, w_in.astype(BF16), BF16, col_scale)
    o = _sb_attention(qkv.reshape(bsz, seq, -1))
    return _out_proj(o.reshape(bsz * seq, -1), w_out.astype(BF16), x)


GDN_HEADS = 8
GDN_HEAD_DIM = 128
GDN_DIM = GDN_HEADS * GDN_HEAD_DIM
GDN_CHUNK = 64
GDN_BLOCK = 128
GDN_CONV = 4
CONV_TAIL_ROWS = 8


def _split2(a):
    hi = a.astype(BF16)
    return hi, (a - hi.astype(F32)).astype(BF16)


def _dot3(a, b):
    ah, al = _split2(a)
    bh, bl = _split2(b)
    return _dot(ah, bh) + (_dot(ah, bl) + _dot(al, bh))


def _bdot3(a, b):
    ah, al = _split2(a)
    bh, bl = _split2(b)
    lhs = jnp.concatenate([ah, ah, al], axis=-1)
    rhs = jnp.concatenate([bh, bl, bh], axis=-2)
    return jnp.einsum("bik,bkj->bij", lhs, rhs, preferred_element_type=F32)


def _dot_exact_rhs(a, b01):
    h0 = a.astype(BF16)
    r1 = a - h0.astype(F32)
    h1 = r1.astype(BF16)
    h2 = (r1 - h1.astype(F32)).astype(BF16)
    return _dot(b01, h0) + (_dot(b01, h1) + _dot(b01, h2))


def _sigmoid(x):
    return 1.0 / (1.0 + jnp.exp(-x))


def _softplus(x):
    return jnp.maximum(x, 0.0) + jnp.log1p(jnp.exp(-jnp.abs(x)))


def _gdn_kernel(main_ref, small_ref, conv_ref, alog_ref, dt_ref, ng_ref, o_ref, state_ref, tail_ref):
    t = GDN_BLOCK
    c = GDN_CHUNK
    d = GDN_HEAD_DIM
    nqkv = 3 * GDN_DIM

    @pl.when(pl.program_id(1) == 0)
    def _():
        state_ref[...] = jnp.zeros_like(state_ref)
        tail_ref[:CONV_TAIL_ROWS, :] = jnp.zeros((CONV_TAIL_ROWS, tail_ref.shape[1]), F32)

    cur = main_ref[:, :nqkv]
    tail_ref[CONV_TAIL_ROWS:, :] = cur
    conv = cur * conv_ref[GDN_CONV - 1:GDN_CONV, :]
    for k in range(1, GDN_CONV):
        conv = conv + tail_ref[CONV_TAIL_ROWS - k:CONV_TAIL_ROWS - k + t, :] * conv_ref[GDN_CONV - 1 - k:GDN_CONV - k, :]
    tail_ref[:CONV_TAIL_ROWS, :] = cur[t - CONV_TAIL_ROWS:, :]
    qkv = conv * _sigmoid(conv)

    small = small_ref[...]
    beta_all = _sigmoid(small)
    g_all = -jnp.exp(alog_ref[...]) * _softplus(small + dt_ref[...])
    r2 = lax.broadcasted_iota(jnp.int32, (t, t), 0)
    c2 = lax.broadcasted_iota(jnp.int32, (t, t), 1)
    same_chunk = (r2 // c) == (c2 // c)
    tril_bd = jnp.where(same_chunk & (c2 <= r2), 1.0, 0.0).astype(BF16)
    gc_all = _dot_exact_rhs(g_all, tril_bd)
    gc_t = gc_all.T

    ri = lax.broadcasted_iota(jnp.int32, (c, c), 0)
    ci = lax.broadcasted_iota(jnp.int32, (c, c), 1)
    incl = (ci <= ri)[None]
    strict = (ci < ri)[None]
    eye = jnp.where(ri == ci, 1.0, 0.0)[None]
    ng = ng_ref[...]
    nch = t // c
    nh = GDN_HEADS

    def stack(x, col0):
        return jnp.stack([x[ch * c:(ch + 1) * c, col0 + h * d:col0 + (h + 1) * d]
                          for ch in range(nch) for h in range(nh)])

    def stack_col(x, col0):
        return jnp.stack([x[ch * c:(ch + 1) * c, col0 + h:col0 + h + 1] for ch in range(nch) for h in range(nh)])

    q = stack(qkv, 0)
    k = stack(qkv, GDN_DIM)
    v = stack(qkv, 2 * GDN_DIM)
    q = q * lax.rsqrt(jnp.sum(q * q, axis=-1, keepdims=True) + NORM_EPS) * (d ** -0.5)
    k = k * lax.rsqrt(jnp.sum(k * k, axis=-1, keepdims=True) + NORM_EPS)
    beta = stack_col(beta_all, 0)
    gc_col = stack_col(gc_all, nh)
    gc_row = jnp.stack([gc_t[nh + h:nh + h + 1, ch * c:(ch + 1) * c]
                        for ch in range(nch) for h in range(nh)])
    gc_last = gc_col[:, c - 1:c, :]
    decay = jnp.where(incl, jnp.exp(jnp.where(incl, gc_col - gc_row, 0.0)), 0.0)
    kb = k.astype(BF16)
    kk = jnp.einsum("bik,bjk->bij", kb, kb, preferred_element_type=F32)
    neg_l = jnp.where(strict, -(beta * kk * decay), 0.0)
    inv = eye + neg_l
    pw = neg_l
    for _ in range(int(math.log2(c)) - 1):
        pw = _bdot3(pw, pw)
        inv = inv + _bdot3(inv, pw)
    e_gc = jnp.exp(gc_col)
    rhs = jnp.concatenate([v * beta, k * (beta * e_gc)], axis=-1)
    sol = _bdot3(inv, rhs)
    u_in, w = sol[:, :, :d], sol[:, :, d:]
    qk = jnp.einsum("bik,bjk->bij", q.astype(BF16), kb, preferred_element_type=F32) * decay
    q_dec = (q * e_gc).astype(BF16)
    k_dec = (k * jnp.exp(gc_last - gc_col)).astype(BF16)
    g_tot = jnp.exp(gc_last)
    wb = w.astype(BF16)
    qkb = qk.astype(BF16)

    state = state_ref[...]
    for ch in range(nch):
        hs = slice(ch * nh, (ch + 1) * nh)
        sb = state.astype(BF16)
        u = u_in[hs] - jnp.einsum("hck,hkv->hcv", wb[hs], sb, preferred_element_type=F32)
        ub = u.astype(BF16)
        o = (jnp.einsum("hck,hkv->hcv", q_dec[hs], sb, preferred_element_type=F32)
             + jnp.einsum("hcj,hjv->hcv", qkb[hs], ub, preferred_element_type=F32))
        kdu = jnp.stack([lax.dot_general(k_dec[ch * nh + h], ub[h], (((0,), (0,)), ((), ())),
                                         preferred_element_type=F32) for h in range(nh)])
        state = state * g_tot[hs] + kdu
        o = o * lax.rsqrt(jnp.mean(o * o, axis=-1, keepdims=True) + NORM_EPS) * ng
        rows = slice(ch * c, (ch + 1) * c)
        z = main_ref[rows, nqkv:nqkv + GDN_DIM]
        o_all = jnp.concatenate([o[h] for h in range(nh)], axis=-1)
        o_ref[rows, :] = (o_all * (z * _sigmoid(z))).astype(o_ref.dtype)
    state_ref[...] = state


def _gdn_core(proj, conv_w, alog_pad, dt_pad, norm_g):
    b, s, _ = proj.shape
    t = GDN_BLOCK
    return pl.pallas_call(
        _gdn_kernel,
        out_shape=jax.ShapeDtypeStruct((b, s, GDN_DIM), BF16),
        grid=(b, s // t),
        in_specs=[
            pl.BlockSpec((None, t, 4 * GDN_DIM), lambda bi, i: (bi, i, 0)),
            pl.BlockSpec((None, t, 128), lambda bi, i: (bi, i, 4 * GDN_DIM // 128)),
            _resident((GDN_CONV, 3 * GDN_DIM)),
            _resident((1, 128)),
            _resident((1, 128)),
            _resident((1, GDN_HEAD_DIM)),
        ],
        out_specs=pl.BlockSpec((None, t, GDN_DIM), lambda bi, i: (bi, i, 0)),
        scratch_shapes=[
            pltpu.VMEM((GDN_HEADS, GDN_HEAD_DIM, GDN_HEAD_DIM), F32),
            pltpu.VMEM((CONV_TAIL_ROWS + GDN_BLOCK, 3 * GDN_DIM), F32),
        ],
        compiler_params=_params("parallel", "arbitrary"),
        name="gdn_core",
    )(proj, proj, conv_w, alog_pad, dt_pad, norm_g)


def _gdn_mixer(x, norm_g, w_in, conv_w, a_log, dt_bias, head_norm, w_out, bsz, seq):
    pad = 128 - 2 * GDN_HEADS
    w = jnp.pad(w_in, ((0, 0), (0, pad))).astype(BF16)
    proj = _norm_proj(x, norm_g, w, F32)
    zeros = jnp.zeros((GDN_HEADS,), F32)
    alog_pad = jnp.pad(jnp.concatenate([zeros, a_log]), (0, pad))[None]
    dt_pad = jnp.pad(jnp.concatenate([zeros, dt_bias]), (0, pad))[None]
    o = _gdn_core(proj.reshape(bsz, seq, -1), conv_w, alog_pad, dt_pad, head_norm[None])
    return _out_proj(o.reshape(bsz * seq, -1), w_out.astype(BF16), x)


DSA_HEADS = 8
DSA_Q_RANK = 384
DSA_KV_RANK = 256
DSA_ROPE_DIM = 32
DSA_NOPE_DIM = 96
DSA_QK_DIM = DSA_ROPE_DIM + DSA_NOPE_DIM
DSA_V_DIM = 128
DSA_IDX_HEADS = 8
DSA_IDX_DIM = 64
DSA_IDX_ROPE = DSA_IDX_DIM // 4
DSA_TOPK_MAX = 256
DSA_LAT = DSA_KV_RANK + DSA_ROPE_DIM
LANES = 128
DSA_TQ = 128
DSA_TK = 512
DSA_HEAD_GROUPS = 4
DSA_COUNT_ACCS = 4
MASKED = -1e30
INT32_MIN = -2 ** 31

_IN_CQ = 0
_IN_CKV = 384
_IN_KROPE = 640
_IN_KROPE_ROT = 768
_IN_KIDX = 896
_IN_KIDX_ROT = 1024
_IN_WIDX = 1152
_IN_COLS = 1280
_TAB_COS = 0
_TAB_SIN = DSA_HEADS * DSA_ROPE_DIM
_TAB_ICOS = 2 * DSA_HEADS * DSA_ROPE_DIM
_TAB_ISIN = _TAB_ICOS + DSA_IDX_HEADS * DSA_IDX_DIM
_TAB_COLS = _TAB_ISIN + DSA_IDX_HEADS * DSA_IDX_DIM


def _dsa_prep_kernel(x_ref, mg_ref, win_ref, cqg_ref, ckvg_ref, kig_ref, kigr_ref, wnope_ref, wrope_ref,
                     wroper_ref, wqi_ref, wqir_ref, wukt_ref, tab_ref,
                     qidx_ref, widx_ref, qlat_ref, kidx_ref, klat_ref):
    h = _rms(x_ref[...], mg_ref[...]).astype(BF16)
    proj = _dot(h, win_ref[...])
    cq = _rms(proj[:, _IN_CQ:_IN_CQ + DSA_Q_RANK], cqg_ref[...]).astype(BF16)
    ckv = _rms(proj[:, _IN_CKV:_IN_CKV + DSA_KV_RANK], ckvg_ref[...])
    tab = tab_ref[...]
    r, di = DSA_ROPE_DIM, DSA_IDX_DIM
    k_rope = (proj[:, _IN_KROPE:_IN_KROPE + r] * tab[:, _TAB_COS:_TAB_COS + r]
              + proj[:, _IN_KROPE_ROT:_IN_KROPE_ROT + r] * tab[:, _TAB_SIN:_TAB_SIN + r])
    klat_ref[:, :DSA_KV_RANK] = ckv.astype(BF16)
    klat_ref[:, DSA_KV_RANK:] = k_rope.astype(BF16)
    kraw = proj[:, _IN_KIDX:_IN_KIDX + di]
    inv = lax.rsqrt(jnp.mean(kraw * kraw, axis=-1, keepdims=True) + NORM_EPS)
    ki = kraw * inv * kig_ref[...]
    kir = proj[:, _IN_KIDX_ROT:_IN_KIDX_ROT + di] * inv * kigr_ref[...]
    kidx_ref[...] = (ki * tab[:, _TAB_ICOS:_TAB_ICOS + di] + kir * tab[:, _TAB_ISIN:_TAB_ISIN + di]).astype(BF16)
    widx_ref[...] = proj[:, _IN_WIDX:_IN_WIDX + LANES] * (DSA_IDX_HEADS ** -0.5 * DSA_IDX_DIM ** -0.5)
    qi = (_dot(cq, wqi_ref[...]) * tab[:, _TAB_ICOS:_TAB_ISIN]
          + _dot(cq, wqir_ref[...]) * tab[:, _TAB_ISIN:_TAB_COLS])
    for hh in range(DSA_IDX_HEADS):
        qidx_ref[hh] = qi[:, hh * di:(hh + 1) * di].astype(BF16)
    q_rope = (_dot(cq, wrope_ref[...]) * tab[:, _TAB_COS:_TAB_SIN]
              + _dot(cq, wroper_ref[...]) * tab[:, _TAB_SIN:_TAB_ICOS])
    q_nope = _dot(cq, wnope_ref[...]).astype(BF16)
    for hh in range(DSA_HEADS):
        q_abs = _dot(q_nope[:, hh * LANES:(hh + 1) * LANES], wukt_ref[hh])
        qlat_ref[hh, :, :DSA_KV_RANK] = q_abs.astype(BF16)
        qlat_ref[hh, :, DSA_KV_RANK:] = q_rope[:, hh * r:(hh + 1) * r].astype(BF16)


def _rot_half(w):
    half = w.shape[-1] // 2
    return jnp.concatenate([-w[..., half:], w[..., :half]], axis=-1)


def _rope_tables(seq):
    pos = jnp.arange(seq, dtype=F32)[:, None]

    def cos_sin(r):
        half = r // 2
        inv_freq = ROPE_THETA ** (-jnp.arange(half, dtype=F32) * (2.0 / r))
        ang = pos * inv_freq[None, :]
        return jnp.cos(ang), jnp.sin(ang)

    c, s = cos_sin(DSA_ROPE_DIM)
    ic, isn = cos_sin(DSA_IDX_ROPE)
    rest = DSA_IDX_DIM - DSA_IDX_ROPE
    icos = jnp.concatenate([ic, ic, jnp.ones((seq, rest), F32)], axis=-1)
    isin = jnp.concatenate([isn, isn, jnp.zeros((seq, rest), F32)], axis=-1)
    return jnp.concatenate([jnp.tile(jnp.concatenate([c, c], -1), (1, DSA_HEADS)),
                            jnp.tile(jnp.concatenate([s, s], -1), (1, DSA_HEADS)),
                            jnp.tile(icos, (1, DSA_IDX_HEADS)), jnp.tile(isin, (1, DSA_IDX_HEADS))], axis=-1)


def _dsa_prep(x, norm_g, w_in, cq_g, ckv_g, kidx_g, w_uq, w_qidx, w_uk, bsz, seq, *, tm=256):
    n, d = x.shape
    o1 = DSA_Q_RANK
    o2 = o1 + DSA_KV_RANK
    o3 = o2 + DSA_ROPE_DIM
    o4 = o3 + DSA_IDX_DIM
    ri = DSA_IDX_ROPE

    def padc(w, cols):
        return jnp.pad(w, ((0, 0), (0, cols - w.shape[1])))

    w_kidx = w_in[:, o3:o4]
    w_ext = jnp.concatenate([
        w_in[:, :o2],
        padc(w_in[:, o2:o3], LANES), padc(_rot_half(w_in[:, o2:o3]), LANES),
        padc(w_kidx, LANES), padc(_rot_half(w_kidx[:, :ri]), LANES),
        padc(w_in[:, o4:], LANES)], axis=1).astype(BF16)
    kig_rot = jnp.pad(jnp.concatenate([kidx_g[ri // 2:ri], kidx_g[:ri // 2]]), (0, DSA_IDX_DIM - ri))
    uq = w_uq.reshape(DSA_Q_RANK, DSA_HEADS, DSA_QK_DIM)
    w_rope = uq[:, :, :DSA_ROPE_DIM]
    w_nope = jnp.pad(uq[:, :, DSA_ROPE_DIM:], ((0, 0), (0, 0), (0, LANES - DSA_NOPE_DIM)))
    qi = w_qidx.reshape(DSA_Q_RANK, DSA_IDX_HEADS, DSA_IDX_DIM)
    qi_rot = jnp.pad(_rot_half(qi[:, :, :ri]), ((0, 0), (0, 0), (0, DSA_IDX_DIM - ri)))
    wukt = jnp.transpose(w_uk.reshape(DSA_KV_RANK, DSA_HEADS, DSA_NOPE_DIM), (1, 2, 0))
    wukt = jnp.pad(wukt, ((0, 0), (0, LANES - DSA_NOPE_DIM), (0, 0))).astype(BF16)
    tab = _rope_tables(seq)
    nblk = seq // tm
    flat = lambda a: a.reshape(DSA_Q_RANK, -1).astype(BF16)
    tok = lambda c: pl.BlockSpec((tm, c), lambda i: (i, 0))
    return pl.pallas_call(
        _dsa_prep_kernel,
        out_shape=(
            jax.ShapeDtypeStruct((DSA_IDX_HEADS, n, DSA_IDX_DIM), BF16),
            jax.ShapeDtypeStruct((n, LANES), F32),
            jax.ShapeDtypeStruct((DSA_HEADS, n, DSA_LAT), BF16),
            jax.ShapeDtypeStruct((n, DSA_IDX_DIM), BF16),
            jax.ShapeDtypeStruct((n, DSA_LAT), BF16),
        ),
        grid=(n // tm,),
        in_specs=[
            tok(d), _resident((1, d)), _resident((d, _IN_COLS)),
            _resident((1, DSA_Q_RANK)), _resident((1, DSA_KV_RANK)),
            _resident((1, DSA_IDX_DIM)), _resident((1, DSA_IDX_DIM)),
            _resident((DSA_Q_RANK, DSA_HEADS * LANES)), _resident((DSA_Q_RANK, DSA_HEADS * DSA_ROPE_DIM)),
            _resident((DSA_Q_RANK, DSA_HEADS * DSA_ROPE_DIM)),
            _resident((DSA_Q_RANK, DSA_IDX_HEADS * DSA_IDX_DIM)), _resident((DSA_Q_RANK, DSA_IDX_HEADS * DSA_IDX_DIM)),
            _resident((DSA_HEADS, LANES, DSA_KV_RANK)),
            pl.BlockSpec((tm, _TAB_COLS), lambda i: (i % nblk, 0)),
        ],
        out_specs=(
            pl.BlockSpec((DSA_IDX_HEADS, tm, DSA_IDX_DIM), lambda i: (0, i, 0)), tok(LANES),
            pl.BlockSpec((DSA_HEADS, tm, DSA_LAT), lambda i: (0, i, 0)),
            tok(DSA_IDX_DIM), tok(DSA_LAT),
        ),
        compiler_params=_params("parallel"),
        name="dsa_prep",
    )(x, norm_g, w_ext, cq_g[None], ckv_g[None], kidx_g[None], kig_rot[None],
      flat(w_nope), flat(w_rope), flat(_rot_half(w_rope)), flat(qi), flat(qi_rot), wukt, tab)


def _dsa_attn_kernel(qidx_ref, widx_ref, qlat_ref, kidx_ref, klat_ref, o_ref,
                     key_ref, bias_ref, m_ref, l_ref, acc_ref, *, topk, scale):
    i = pl.program_id(1)
    tq, tk = DSA_TQ, DSA_TK
    sub = tk // LANES
    nsteps = (i * tq + tq + tk - 1) // tk
    w_t = widx_ref[...].T
    qpos = i * tq + lax.broadcasted_iota(jnp.int32, (1, tq), 1)
    krow = lax.broadcasted_iota(jnp.int32, (LANES, tq), 0)
    q_idx = qidx_ref[...].reshape(DSA_IDX_HEADS * tq, DSA_IDX_DIM)

    def score_step(jj, _):
        base = pl.multiple_of(jj * tk, tk)
        lg = _dot_nt(kidx_ref[pl.ds(base, tk), :], q_idx)
        for u in range(sub):
            rows = slice(u * LANES, (u + 1) * LANES)
            s = w_t[0:1, :] * jnp.maximum(lg[rows, 0:tq], 0.0)
            for h in range(1, DSA_IDX_HEADS):
                s = s + w_t[h:h + 1, :] * jnp.maximum(lg[rows, h * tq:(h + 1) * tq], 0.0)
            s = jnp.where(s == 0.0, 0.0, s)
            bits = lax.bitcast_convert_type(s, jnp.int32)
            key = bits ^ ((bits >> 31) & 0x7FFFFFFF)
            key = jnp.where(jj * tk + u * LANES + krow <= qpos, key, INT32_MIN)
            key_ref[pl.ds(pl.multiple_of(base + u * LANES, LANES), LANES), :] = key
        return 0

    lax.fori_loop(0, nsteps, score_step, 0)

    nacc = DSA_COUNT_ACCS

    def count_ge(trial):
        def body(jj, acc):
            blk = key_ref[pl.ds(pl.multiple_of(jj * tk, tk), tk), :]
            hit = jnp.where(blk >= trial, 1.0, 0.0)
            return acc + jnp.sum(hit.reshape(tk // (8 * nacc), nacc * 8, tq), axis=0)
        acc = lax.fori_loop(0, nsteps, body, jnp.zeros((nacc * 8, tq), F32))
        return jnp.sum(acc, axis=0, keepdims=True)

    def bit_step(it, r):
        trial = r + jnp.left_shift(jnp.int32(1), 31 - it)
        return jnp.where(count_ge(trial) >= topk, trial, r)

    kth = lax.fori_loop(0, 32, bit_step, jnp.full((1, tq), INT32_MIN, jnp.int32))
    n_ge = count_ge(kth)
    n_gt = count_ge(kth + 1)
    need = topk - n_gt
    valid_min = INT32_MIN + 1
    has_tie = jnp.where((n_ge > topk) & (kth >= valid_min), 1.0, 0.0)
    tie_any = jnp.max(has_tie) > 0.0

    def write_bias(jj, u, sel):
        bias_ref[jj, :, u * LANES:(u + 1) * LANES] = jnp.where(sel, 0.0, MASKED).T

    @pl.when(jnp.logical_not(tie_any))
    def _():
        thr = jnp.maximum(kth, valid_min)

        def body(jj, _):
            for u in range(sub):
                blk = key_ref[pl.ds(pl.multiple_of((jj * sub + u) * LANES, LANES), LANES), :]
                write_bias(jj, u, blk >= thr)
            return 0

        lax.fori_loop(0, nsteps, body, 0)

    @pl.when(tie_any)
    def _():
        r2 = lax.broadcasted_iota(jnp.int32, (LANES, LANES), 0)
        c2 = lax.broadcasted_iota(jnp.int32, (LANES, LANES), 1)
        before = jnp.where(c2 < r2, 1.0, 0.0).astype(BF16)

        def body(jj, run):
            for u in range(sub):
                blk = key_ref[pl.ds(pl.multiple_of((jj * sub + u) * LANES, LANES), LANES), :]
                eq = jnp.where((blk == kth) & (blk >= valid_min), 1.0, 0.0)
                rank = _dot(before, eq.astype(BF16)) + run
                take = jnp.where(rank < need, eq, 0.0)
                write_bias(jj, u, jnp.where(blk > kth, 1.0, take) > 0.0)
                run = run + jnp.sum(eq, axis=0, keepdims=True)
            return run

        lax.fori_loop(0, nsteps, body, jnp.zeros((1, tq), F32))

    ng = DSA_HEAD_GROUPS
    gh = DSA_HEADS // ng
    gr = gh * tq
    qs = [qlat_ref[g * gh:(g + 1) * gh].reshape(gr, DSA_LAT) for g in range(ng)]
    m_ref[...] = jnp.full_like(m_ref, MASKED)
    l_ref[...] = jnp.zeros_like(l_ref)
    acc_ref[...] = jnp.zeros_like(acc_ref)

    def attn_step(jj, _):
        kt = klat_ref[pl.ds(pl.multiple_of(jj * tk, tk), tk), :]
        kv = kt[:, :DSA_KV_RANK]
        bias = bias_ref[jj][None]
        gs = range(ng)
        rows = [slice(g * gr, (g + 1) * gr) for g in gs]
        att = [_dot_nt(qs[g], kt) * scale for g in gs]
        att = [(a.reshape(gh, tq, tk) + bias).reshape(gr, tk) for a in att]
        m_old = [m_ref[rows[g], :] for g in gs]
        m_new = [jnp.maximum(m_old[g], jnp.max(att[g], axis=-1, keepdims=True)) for g in gs]
        p = [jnp.exp(att[g] - m_new[g]) for g in gs]
        alpha = [jnp.exp(m_old[g] - m_new[g]) for g in gs]
        pv = [_dot(p[g].astype(BF16), kv) for g in gs]
        for g in gs:
            l_ref[rows[g], :] = alpha[g] * l_ref[rows[g], :] + jnp.sum(p[g], axis=-1, keepdims=True)
            acc_ref[rows[g], :] = alpha[g] * acc_ref[rows[g], :] + pv[g]
            m_ref[rows[g], :] = m_new[g]
        return 0

    lax.fori_loop(0, nsteps, attn_step, 0)
    o = acc_ref[...] / l_ref[...]
    o_ref[...] = o.reshape(DSA_HEADS, tq, DSA_KV_RANK).astype(o_ref.dtype)


def _dsa_attn(qidx, widx, qlat, kidx, klat, topk):
    _, b, s, _ = qidx.shape
    tq = DSA_TQ
    return pl.pallas_call(
        functools.partial(_dsa_attn_kernel, topk=topk, scale=DSA_QK_DIM ** -0.5),
        out_shape=jax.ShapeDtypeStruct((DSA_HEADS, b, s, DSA_KV_RANK), BF16),
        grid=(b, s // tq),
        in_specs=[
            pl.BlockSpec((DSA_IDX_HEADS, None, tq, DSA_IDX_DIM), lambda bi, i: (0, bi, i, 0)),
            pl.BlockSpec((None, tq, LANES), lambda bi, i: (bi, i, 0)),
            pl.BlockSpec((DSA_HEADS, None, tq, DSA_LAT), lambda bi, i: (0, bi, i, 0)),
            pl.BlockSpec((None, s, DSA_IDX_DIM), lambda bi, i: (bi, 0, 0)),
            pl.BlockSpec((None, s, DSA_LAT), lambda bi, i: (bi, 0, 0)),
        ],
        out_specs=pl.BlockSpec((DSA_HEADS, None, tq, DSA_KV_RANK), lambda bi, i: (0, bi, i, 0)),
        scratch_shapes=[
            pltpu.VMEM((s, tq), jnp.int32),
            pltpu.VMEM((s // DSA_TK, tq, DSA_TK), F32),
            pltpu.VMEM((DSA_HEADS * tq, 1), F32),
            pltpu.VMEM((DSA_HEADS * tq, 1), F32),
            pltpu.VMEM((DSA_HEADS * tq, DSA_KV_RANK), F32),
        ],
        compiler_params=_params("parallel", "arbitrary"),
        name="dsa_attn",
    )(qidx, widx, qlat, kidx, klat)


def _dsa_out_kernel(olat_ref, wuv_ref, wout_ref, x_ref, y_ref):
    parts = [_dot(olat_ref[h], wuv_ref[h]) for h in range(DSA_HEADS)]
    o = jnp.concatenate(parts, axis=-1).astype(BF16)
    y_ref[...] = x_ref[...] + _dot(o, wout_ref[...])


def _dsa_out(olat, w_uv, w_out, x, *, tm=512):
    n, d = x.shape
    wuv = jnp.transpose(w_uv.reshape(DSA_KV_RANK, DSA_HEADS, DSA_V_DIM), (1, 0, 2)).astype(BF16)
    return pl.pallas_call(
        _dsa_out_kernel,
        out_shape=jax.ShapeDtypeStruct((n, d), F32),
        grid=(n // tm,),
        in_specs=[
            pl.BlockSpec((DSA_HEADS, tm, DSA_KV_RANK), lambda i: (0, i, 0)),
            _resident((DSA_HEADS, DSA_KV_RANK, DSA_V_DIM)),
            _resident((DSA_HEADS * DSA_V_DIM, d)),
            pl.BlockSpec((tm, d), lambda i: (i, 0)),
        ],
        out_specs=pl.BlockSpec((tm, d), lambda i: (i, 0)),
        compiler_params=_params("parallel"),
        name="dsa_out",
    )(olat, wuv, w_out.astype(BF16), x)


def _dsa_mixer(x, norm_g, w_in, cq_g, ckv_g, kidx_g, w_uq, w_qidx, w_uk, w_uv, w_out, bsz, seq):
    qidx, widx, qlat, kidx, klat = _dsa_prep(x, norm_g, w_in, cq_g, ckv_g, kidx_g, w_uq, w_qidx, w_uk, bsz, seq)
    topk = min(DSA_TOPK_MAX, seq // 4)
    b3 = lambda a: a.reshape(bsz, seq, -1)
    olat = _dsa_attn(qidx.reshape(DSA_IDX_HEADS, bsz, seq, DSA_IDX_DIM), b3(widx),
                     qlat.reshape(DSA_HEADS, bsz, seq, DSA_LAT), b3(kidx), b3(klat), topk)
    return _dsa_out(olat.reshape(DSA_HEADS, bsz * seq, DSA_KV_RANK), w_uv, w_out, x)


def kernel(x, ffn1_norm, ffn1_w_gu, ffn1_w_down, mix_norm, ffn2_norm, ffn2_w_gu, ffn2_w_down, gdn_w_in, gdn_conv, gdn_a_log, gdn_dt_bias, gdn_norm, gdn_w_out, sb_w_in, sb_w_out, dsa_w_in, dsa_cq_norm, dsa_ckv_norm, dsa_kidx_norm, dsa_w_uq, dsa_w_qidx, dsa_w_uk, dsa_w_uv, dsa_w_out, final_norm):
    bsz, seq, d = x.shape
    d_ff = ffn1_w_down.shape[1]
    xf = x.reshape(bsz * seq, d)

    def ffn(xf, g, w_gu, w_down):
        return _ffn(xf, g[None], w_gu[:, :d_ff].astype(BF16), w_gu[:, d_ff:].astype(BF16), w_down.astype(BF16))

    for i in range(ffn1_norm.shape[0]):
        kind, j = i % N_MIXERS, i // N_MIXERS
        xf = ffn(xf, ffn1_norm[i], ffn1_w_gu[i], ffn1_w_down[i])
        g = mix_norm[i][None]
        if kind == 0:
            xf = _gdn_mixer(xf, g, gdn_w_in[j], gdn_conv[j], gdn_a_log[j], gdn_dt_bias[j], gdn_norm[j],
                            gdn_w_out[j], bsz, seq)
        elif kind == 1:
            xf = _stick_breaking_mixer(xf, g, sb_w_in[j], sb_w_out[j], bsz, seq)
        else:
            xf = _dsa_mixer(xf, g, dsa_w_in[j], dsa_cq_norm[j], dsa_ckv_norm[j], dsa_kidx_norm[j], dsa_w_uq[j],
                            dsa_w_qidx[j], dsa_w_uk[j], dsa_w_uv[j], dsa_w_out[j], bsz, seq)
        xf = ffn(xf, ffn2_norm[i], ffn2_w_gu[i], ffn2_w_down[i])
    return _final_norm(xf, final_norm[None]).reshape(bsz, seq, d)
```

```python
import functools
import math

import jax
import jax.numpy as jnp
from jax import lax
from jax.experimental import pallas as pl
from jax.experimental.pallas import tpu as pltpu

F32 = jnp.float32
BF16 = jnp.bfloat16

NORM_EPS = 1e-6
ROPE_THETA = 500000.0
V7X_VMEM_LIMIT_BYTES = 56 * 1024 * 1024

N_MIXERS = 3
SB_HEADS = 8
SB_HEAD_DIM = 128


def _params(*sem):
    return pltpu.CompilerParams(dimension_semantics=sem, vmem_limit_bytes=V7X_VMEM_LIMIT_BYTES)


def _dot(a, b):
    return jnp.dot(a, b, preferred_element_type=F32)


def _dot_nt(a, b):
    return lax.dot_general(a, b, (((1,), (1,)), ((), ())), preferred_element_type=F32)


def _rms(x, g):
    return x * lax.rsqrt(jnp.mean(x * x, axis=-1, keepdims=True) + NORM_EPS) * g


def _resident(shape):
    return pl.BlockSpec(shape, lambda *_: (0,) * len(shape), pipeline_mode=pl.Buffered(1))


def _ffn_kernel(*refs, fc, pre, post_norm):
    refs = list(refs)
    x_ref = refs.pop(0)
    x = x_ref[...]
    if pre == "proj":
        o_ref_in, wout_ref = refs.pop(0), refs.pop(0)
        x = x + _dot(o_ref_in[...], wout_ref[...])
    elif pre == "dsa":
        olat_ref, wuv_ref, wout_ref = refs.pop(0), refs.pop(0), refs.pop(0)
        parts = [_dot(olat_ref[h], wuv_ref[h]) for h in range(olat_ref.shape[0])]
        x = x + _dot(jnp.concatenate(parts, axis=-1).astype(BF16), wout_ref[...])
    g_ref, wg_ref, wu_ref, wd_ref = refs[:4]
    o_ref, acc_ref = refs[-2:]
    xn = _rms(x, g_ref[...]).astype(BF16)
    d_ff = wg_ref.shape[1]
    for c in range(d_ff // fc):
        sl = slice(c * fc, (c + 1) * fc)
        gate = _dot(xn, wg_ref[:, sl])
        up = _dot(xn, wu_ref[:, sl])
        h = (gate / (1.0 + jnp.exp(-gate)) * up).astype(BF16)
        y = _dot(h, wd_ref[sl, :])
        if c == 0:
            acc_ref[...] = y
        else:
            acc_ref[...] += y
    y = x + 0.5 * acc_ref[...]
    if post_norm:
        y = _rms(y, refs[4][...])
    o_ref[...] = y


def _ffn(x, g, wg, wu, wd, *, pre=None, pre_args=(), final_g=None, tm=512, fc=256):
    n, d = x.shape
    d_ff = wg.shape[1]
    tok = lambda c: pl.BlockSpec((tm, c), lambda i: (i, 0))
    pre_specs = []
    if pre == "proj":
        o, w_out = pre_args
        pre_specs = [tok(o.shape[1]), _resident(w_out.shape)]
    elif pre == "dsa":
        olat, wuv, w_out = pre_args
        pre_specs = [pl.BlockSpec((olat.shape[0], tm, olat.shape[2]), lambda i: (0, i, 0)),
                     _resident(wuv.shape), _resident(w_out.shape)]
    post = [] if final_g is None else [final_g]
    return pl.pallas_call(
        functools.partial(_ffn_kernel, fc=fc, pre=pre, post_norm=final_g is not None),
        out_shape=jax.ShapeDtypeStruct((n, d), F32),
        grid=(n // tm,),
        in_specs=[tok(d)] + pre_specs + [_resident((1, d)), _resident((d, d_ff)), _resident((d, d_ff)),
                                          _resident((d_ff, d))] + [_resident((1, d))] * len(post),
        out_specs=tok(d),
        scratch_shapes=[pltpu.VMEM((tm, d), F32)],
        compiler_params=_params("parallel"),
        name="ffn",
    )(x, *pre_args, g, wg, wu, wd, *post)


def _norm_proj_kernel(x_ref, g_ref, w_ref, *rest):
    o_ref = rest[-1]
    xn = _rms(x_ref[...], g_ref[...]).astype(BF16)
    y = _dot(xn, w_ref[...])
    if len(rest) == 2:
        y = y * rest[0][...]
    o_ref[...] = y.astype(o_ref.dtype)


def _norm_proj(x, g, w, out_dtype, col_scale=None, *, tm=512):
    n, d = x.shape
    m = w.shape[1]
    extra = [] if col_scale is None else [col_scale]
    return pl.pallas_call(
        _norm_proj_kernel,
        out_shape=jax.ShapeDtypeStruct((n, m), out_dtype),
        grid=(n // tm,),
        in_specs=[pl.BlockSpec((tm, d), lambda i: (i, 0)), _resident((1, d)), _resident((d, m))]
        + [_resident((1, m))] * len(extra),
        out_specs=pl.BlockSpec((tm, m), lambda i: (i, 0)),
        compiler_params=_params("parallel"),
        name="norm_proj",
    )(x, g, w, *extra)


SB_LINEAR_ABOVE = 30.0
SB_HEADS_PER_STEP = 8


def _sb_kernel(q_ref, k_ref, v_ref, o_ref, *, t):
    i = pl.program_id(2)
    d = SB_HEAD_DIM
    nh = SB_HEADS_PER_STEP
    row = lax.broadcasted_iota(jnp.int32, (t, t), 0)
    col = lax.broadcasted_iota(jnp.int32, (t, t), 1)
    tri = jnp.where(row >= col, 1.0, 0.0).astype(BF16)
    earlier = col < row
    qs = [q_ref[:, h * d:(h + 1) * d] for h in range(nh)]

    def tiles(j, st, diag):
        ks = pl.ds(pl.multiple_of(j * t, t), t)
        hs = range(nh)
        zs = [_dot_nt(qs[h], k_ref[ks, h * d:(h + 1) * d]) for h in hs]
        sps = [jnp.where(z > SB_LINEAR_ABOVE, z, jnp.log2(1.0 + jnp.exp2(z))) for z in zs]
        if diag:
            sps = [jnp.where(earlier, sp, 0.0) for sp in sps]
        rins = [_dot(sp.astype(BF16), tri) for sp in sps]
        atts = [jnp.exp2(zs[h] - rins[h] - st[h][0]) for h in hs]
        if diag:
            atts = [jnp.where(earlier, a, 0.0) for a in atts]
        accs = [st[h][1] + _dot(atts[h].astype(BF16), v_ref[ks, h * d:(h + 1) * d]) for h in hs]
        return tuple((st[h][0] + rins[h][:, 0:1], accs[h]) for h in hs)

    state = tiles(i, tuple((jnp.zeros((t, 1), F32), jnp.zeros((t, d), F32)) for _ in range(nh)), True)
    state = lax.fori_loop(0, i, lambda s, st: tiles(i - 1 - s, st, False), state)
    for h in range(nh):
        o_ref[:, h * d:(h + 1) * d] = state[h][1].astype(o_ref.dtype)


def _sb_attention(qkv, *, t=256):
    b, s, _ = qkv.shape
    h, d = SB_HEADS, SB_HEAD_DIM
    nh = SB_HEADS_PER_STEP
    w = nh * d
    ng = h // nh
    return pl.pallas_call(
        functools.partial(_sb_kernel, t=t),
        out_shape=jax.ShapeDtypeStruct((b, s, h * d), BF16),
        grid=(b, ng, s // t),
        in_specs=[
            pl.BlockSpec((None, t, w), lambda bi, gi, i: (bi, i, gi)),
            pl.BlockSpec((None, s, w), lambda bi, gi, i: (bi, 0, ng + gi)),
            pl.BlockSpec((None, s, w), lambda bi, gi, i: (bi, 0, 2 * ng + gi)),
        ],
        out_specs=pl.BlockSpec((None, t, w), lambda bi, gi, i: (bi, i, gi)),
        compiler_params=_params("parallel", "parallel", "arbitrary"),
        name="sb_attention",
    )(qkv, qkv, qkv)


def _stick_breaking_mixer(x, norm_g, w_in, w_out, bsz, seq):
    nq = SB_HEADS * SB_HEAD_DIM
    col_scale = jnp.concatenate([jnp.full((nq,), SB_HEAD_DIM ** -0.5 * math.log2(math.e), F32),
                                 jnp.ones((2 * nq,), F32)])[None]
    qkv = _norm_proj(x, norm_g, w_in.astype(BF16), BF16, col_scale)
    o = _sb_attention(qkv.reshape(bsz, seq, -1))
    return "proj", (o.reshape(bsz * seq, -1), w_out.astype(BF16))


GDN_HEADS = 8
GDN_HEAD_DIM = 128
GDN_DIM = GDN_HEADS * GDN_HEAD_DIM
GDN_CHUNK = 64
GDN_BLOCK = 128
GDN_CONV = 4
CONV_TAIL_ROWS = 8


def _split2(a):
    hi = a.astype(BF16)
    return hi, (a - hi.astype(F32)).astype(BF16)


def _bdot3(a_parts, b_parts):
    ah, al = a_parts
    bh, bl = b_parts
    lhs = jnp.concatenate([ah, ah, al], axis=-1)
    rhs = jnp.concatenate([bh, bl, bh], axis=-2)
    return jnp.einsum("bik,bkj->bij", lhs, rhs, preferred_element_type=F32)


def _dot_exact_rhs(a, b01):
    h0 = a.astype(BF16)
    r1 = a - h0.astype(F32)
    h1 = r1.astype(BF16)
    h2 = (r1 - h1.astype(F32)).astype(BF16)
    return _dot(b01, h0) + (_dot(b01, h1) + _dot(b01, h2))


def _sigmoid(x):
    return 1.0 / (1.0 + jnp.exp(-x))


def _softplus(x):
    return jnp.maximum(x, 0.0) + jnp.log1p(jnp.exp(-jnp.abs(x)))


def _gdn_kernel(main_ref, small_ref, conv_ref, alog_ref, dt_ref, ng_ref, o_ref, state_ref, tail_ref):
    t = GDN_BLOCK
    c = GDN_CHUNK
    d = GDN_HEAD_DIM
    nqkv = 3 * GDN_DIM

    @pl.when(pl.program_id(1) == 0)
    def _():
        state_ref[...] = jnp.zeros_like(state_ref)
        tail_ref[:CONV_TAIL_ROWS, :] = jnp.zeros((CONV_TAIL_ROWS, tail_ref.shape[1]), F32)

    cur = main_ref[:, :nqkv]
    tail_ref[CONV_TAIL_ROWS:, :] = cur
    conv = cur * conv_ref[GDN_CONV - 1:GDN_CONV, :]
    for k in range(1, GDN_CONV):
        conv = conv + tail_ref[CONV_TAIL_ROWS - k:CONV_TAIL_ROWS - k + t, :] * conv_ref[GDN_CONV - 1 - k:GDN_CONV - k, :]
    tail_ref[:CONV_TAIL_ROWS, :] = cur[t - CONV_TAIL_ROWS:, :]
    qkv = conv * _sigmoid(conv)

    small = small_ref[...]
    beta_all = _sigmoid(small)
    g_all = -jnp.exp(alog_ref[...]) * _softplus(small + dt_ref[...])
    r2 = lax.broadcasted_iota(jnp.int32, (t, t), 0)
    c2 = lax.broadcasted_iota(jnp.int32, (t, t), 1)
    same_chunk = (r2 // c) == (c2 // c)
    tril_bd = jnp.where(same_chunk & (c2 <= r2), 1.0, 0.0).astype(BF16)
    gc_all = _dot_exact_rhs(g_all, tril_bd)
    gc_t = gc_all.T

    ri = lax.broadcasted_iota(jnp.int32, (c, c), 0)
    ci = lax.broadcasted_iota(jnp.int32, (c, c), 1)
    incl = (ci <= ri)[None]
    strict = (ci < ri)[None]
    eye = jnp.where(ri == ci, 1.0, 0.0)[None]
    ng = ng_ref[...]
    nch = t // c
    nh = GDN_HEADS

    def stack(x, col0):
        return jnp.stack([x[ch * c:(ch + 1) * c, col0 + h * d:col0 + (h + 1) * d]
                          for ch in range(nch) for h in range(nh)])

    def stack_col(x, col0):
        return jnp.stack([x[ch * c:(ch + 1) * c, col0 + h:col0 + h + 1] for ch in range(nch) for h in range(nh)])

    q = stack(qkv, 0)
    k = stack(qkv, GDN_DIM)
    v = stack(qkv, 2 * GDN_DIM)
    q = q * lax.rsqrt(jnp.sum(q * q, axis=-1, keepdims=True) + NORM_EPS) * (d ** -0.5)
    k = k * lax.rsqrt(jnp.sum(k * k, axis=-1, keepdims=True) + NORM_EPS)
    beta = stack_col(beta_all, 0)
    gc_col = stack_col(gc_all, nh)
    gc_row = jnp.stack([gc_t[nh + h:nh + h + 1, ch * c:(ch + 1) * c]
                        for ch in range(nch) for h in range(nh)])
    gc_last = gc_col[:, c - 1:c, :]
    decay = jnp.where(incl, jnp.exp(jnp.where(incl, gc_col - gc_row, 0.0)), 0.0)
    kb = k.astype(BF16)
    kk = jnp.einsum("bik,bjk->bij", kb, kb, preferred_element_type=F32)
    neg_l = jnp.where(strict, -(beta * kk * decay), 0.0)
    inv = eye + neg_l
    pw_parts = _split2(neg_l)
    for _ in range(int(math.log2(c)) - 1):
        pw_parts = _split2(_bdot3(pw_parts, pw_parts))
        inv = inv + _bdot3(_split2(inv), pw_parts)
    e_gc = jnp.exp(gc_col)
    rhs = jnp.concatenate([v * beta, k * (beta * e_gc)], axis=-1)
    sol = _bdot3(_split2(inv), _split2(rhs))
    u_in, w = sol[:, :, :d], sol[:, :, d:]
    qk = jnp.einsum("bik,bjk->bij", q.astype(BF16), kb, preferred_element_type=F32) * decay
    q_dec = (q * e_gc).astype(BF16)
    k_dec = (k * jnp.exp(gc_last - gc_col)).astype(BF16)
    g_tot = jnp.exp(gc_last)
    wb = w.astype(BF16)
    qkb = qk.astype(BF16)

    state = state_ref[...]
    for ch in range(nch):
        hs = slice(ch * nh, (ch + 1) * nh)
        sb = state.astype(BF16)
        u = u_in[hs] - jnp.einsum("hck,hkv->hcv", wb[hs], sb, preferred_element_type=F32)
        ub = u.astype(BF16)
        o = (jnp.einsum("hck,hkv->hcv", q_dec[hs], sb, preferred_element_type=F32)
             + jnp.einsum("hcj,hjv->hcv", qkb[hs], ub, preferred_element_type=F32))
        kdu = jnp.stack([lax.dot_general(k_dec[ch * nh + h], ub[h], (((0,), (0,)), ((), ())),
                                         preferred_element_type=F32) for h in range(nh)])
        state = state * g_tot[hs] + kdu
        o = o * lax.rsqrt(jnp.mean(o * o, axis=-1, keepdims=True) + NORM_EPS) * ng
        rows = slice(ch * c, (ch + 1) * c)
        z = main_ref[rows, nqkv:nqkv + GDN_DIM]
        o_all = jnp.concatenate([o[h] for h in range(nh)], axis=-1)
        o_ref[rows, :] = (o_all * (z * _sigmoid(z))).astype(o_ref.dtype)
    state_ref[...] = state


def _gdn_core(proj, conv_w, alog_pad, dt_pad, norm_g):
    b, s, _ = proj.shape
    t = GDN_BLOCK
    return pl.pallas_call(
        _gdn_kernel,
        out_shape=jax.ShapeDtypeStruct((b, s, GDN_DIM), BF16),
        grid=(b, s // t),
        in_specs=[
            pl.BlockSpec((None, t, 4 * GDN_DIM), lambda bi, i: (bi, i, 0)),
            pl.BlockSpec((None, t, 128), lambda bi, i: (bi, i, 4 * GDN_DIM // 128)),
            _resident((GDN_CONV, 3 * GDN_DIM)),
            _resident((1, 128)),
            _resident((1, 128)),
            _resident((1, GDN_HEAD_DIM)),
        ],
        out_specs=pl.BlockSpec((None, t, GDN_DIM), lambda bi, i: (bi, i, 0)),
        scratch_shapes=[
            pltpu.VMEM((GDN_HEADS, GDN_HEAD_DIM, GDN_HEAD_DIM), F32),
            pltpu.VMEM((CONV_TAIL_ROWS + GDN_BLOCK, 3 * GDN_DIM), F32),
        ],
        compiler_params=_params("parallel", "arbitrary"),
        name="gdn_core",
    )(proj, proj, conv_w, alog_pad, dt_pad, norm_g)


def _gdn_mixer(x, norm_g, w_in, conv_w, a_log, dt_bias, head_norm, w_out, bsz, seq):
    pad = 128 - 2 * GDN_HEADS
    w = jnp.pad(w_in, ((0, 0), (0, pad))).astype(BF16)
    proj = _norm_proj(x, norm_g, w, F32)
    zeros = jnp.zeros((GDN_HEADS,), F32)
    alog_pad = jnp.pad(jnp.concatenate([zeros, a_log]), (0, pad))[None]
    dt_pad = jnp.pad(jnp.concatenate([zeros, dt_bias]), (0, pad))[None]
    o = _gdn_core(proj.reshape(bsz, seq, -1), conv_w, alog_pad, dt_pad, head_norm[None])
    return "proj", (o.reshape(bsz * seq, -1), w_out.astype(BF16))


DSA_HEADS = 8
DSA_Q_RANK = 384
DSA_KV_RANK = 256
DSA_ROPE_DIM = 32
DSA_NOPE_DIM = 96
DSA_QK_DIM = DSA_ROPE_DIM + DSA_NOPE_DIM
DSA_V_DIM = 128
DSA_IDX_HEADS = 8
DSA_IDX_DIM = 64
DSA_IDX_ROPE = DSA_IDX_DIM // 4
DSA_TOPK_MAX = 256
DSA_LAT = DSA_KV_RANK + DSA_ROPE_DIM
LANES = 128
DSA_TQ = 128
DSA_TK = 512
DSA_HEAD_GROUPS = 4
DSA_COUNT_ACCS = 4
MASKED = -1e30
INT32_MIN = -2 ** 31

_IN_CQ = 0
_IN_CKV = 384
_IN_KROPE = 640
_IN_KROPE_ROT = 768
_IN_KIDX = 896
_IN_KIDX_ROT = 1024
_IN_WIDX = 1152
_IN_COLS = 1280
_TAB_COS = 0
_TAB_SIN = DSA_HEADS * DSA_ROPE_DIM
_TAB_ICOS = 2 * DSA_HEADS * DSA_ROPE_DIM
_TAB_ISIN = _TAB_ICOS + DSA_IDX_HEADS * DSA_IDX_DIM
_TAB_COLS = _TAB_ISIN + DSA_IDX_HEADS * DSA_IDX_DIM


def _dsa_prep_kernel(x_ref, mg_ref, win_ref, cqg_ref, ckvg_ref, kig_ref, kigr_ref, wnope_ref, wrope_ref,
                     wroper_ref, wqi_ref, wqir_ref, wukt_ref, tab_ref,
                     qidx_ref, widx_ref, qlat_ref, kidx_ref, klat_ref):
    h = _rms(x_ref[...], mg_ref[...]).astype(BF16)
    proj = _dot(h, win_ref[...])
    cq = _rms(proj[:, _IN_CQ:_IN_CQ + DSA_Q_RANK], cqg_ref[...]).astype(BF16)
    ckv = _rms(proj[:, _IN_CKV:_IN_CKV + DSA_KV_RANK], ckvg_ref[...])
    tab = tab_ref[...]
    r, di = DSA_ROPE_DIM, DSA_IDX_DIM
    k_rope = (proj[:, _IN_KROPE:_IN_KROPE + r] * tab[:, _TAB_COS:_TAB_COS + r]
              + proj[:, _IN_KROPE_ROT:_IN_KROPE_ROT + r] * tab[:, _TAB_SIN:_TAB_SIN + r])
    klat_ref[:, :DSA_KV_RANK] = ckv.astype(BF16)
    klat_ref[:, DSA_KV_RANK:] = k_rope.astype(BF16)
    kraw = proj[:, _IN_KIDX:_IN_KIDX + di]
    inv = lax.rsqrt(jnp.mean(kraw * kraw, axis=-1, keepdims=True) + NORM_EPS)
    ki = kraw * inv * kig_ref[...]
    kir = proj[:, _IN_KIDX_ROT:_IN_KIDX_ROT + di] * inv * kigr_ref[...]
    kidx_ref[...] = (ki * tab[:, _TAB_ICOS:_TAB_ICOS + di] + kir * tab[:, _TAB_ISIN:_TAB_ISIN + di]).astype(BF16)
    widx_ref[...] = proj[:, _IN_WIDX:_IN_WIDX + LANES] * (DSA_IDX_HEADS ** -0.5 * DSA_IDX_DIM ** -0.5)
    qi = (_dot(cq, wqi_ref[...]) * tab[:, _TAB_ICOS:_TAB_ISIN]
          + _dot(cq, wqir_ref[...]) * tab[:, _TAB_ISIN:_TAB_COLS])
    for hh in range(DSA_IDX_HEADS):
        qidx_ref[hh] = qi[:, hh * di:(hh + 1) * di].astype(BF16)
    q_rope = (_dot(cq, wrope_ref[...]) * tab[:, _TAB_COS:_TAB_SIN]
              + _dot(cq, wroper_ref[...]) * tab[:, _TAB_SIN:_TAB_ICOS])
    q_nope = _dot(cq, wnope_ref[...]).astype(BF16)
    for hh in range(DSA_HEADS):
        q_abs = _dot(q_nope[:, hh * LANES:(hh + 1) * LANES], wukt_ref[hh])
        qlat_ref[hh, :, :DSA_KV_RANK] = q_abs.astype(BF16)
        qlat_ref[hh, :, DSA_KV_RANK:] = q_rope[:, hh * r:(hh + 1) * r].astype(BF16)


def _rot_half(w):
    half = w.shape[-1] // 2
    return jnp.concatenate([-w[..., half:], w[..., :half]], axis=-1)


def _rope_tables(seq):
    pos = jnp.arange(seq, dtype=F32)[:, None]

    def cos_sin(r):
        half = r // 2
        inv_freq = ROPE_THETA ** (-jnp.arange(half, dtype=F32) * (2.0 / r))
        ang = pos * inv_freq[None, :]
        return jnp.cos(ang), jnp.sin(ang)

    c, s = cos_sin(DSA_ROPE_DIM)
    ic, isn = cos_sin(DSA_IDX_ROPE)
    rest = DSA_IDX_DIM - DSA_IDX_ROPE
    icos = jnp.concatenate([ic, ic, jnp.ones((seq, rest), F32)], axis=-1)
    isin = jnp.concatenate([isn, isn, jnp.zeros((seq, rest), F32)], axis=-1)
    return jnp.concatenate([jnp.tile(jnp.concatenate([c, c], -1), (1, DSA_HEADS)),
                            jnp.tile(jnp.concatenate([s, s], -1), (1, DSA_HEADS)),
                            jnp.tile(icos, (1, DSA_IDX_HEADS)), jnp.tile(isin, (1, DSA_IDX_HEADS))], axis=-1)


def _dsa_prep(x, norm_g, w_in, cq_g, ckv_g, kidx_g, w_uq, w_qidx, w_uk, bsz, seq, *, tm=256):
    n, d = x.shape
    o1 = DSA_Q_RANK
    o2 = o1 + DSA_KV_RANK
    o3 = o2 + DSA_ROPE_DIM
    o4 = o3 + DSA_IDX_DIM
    ri = DSA_IDX_ROPE

    def padc(w, cols):
        return jnp.pad(w, ((0, 0), (0, cols - w.shape[1])))

    w_kidx = w_in[:, o3:o4]
    w_ext = jnp.concatenate([
        w_in[:, :o2],
        padc(w_in[:, o2:o3], LANES), padc(_rot_half(w_in[:, o2:o3]), LANES),
        padc(w_kidx, LANES), padc(_rot_half(w_kidx[:, :ri]), LANES),
        padc(w_in[:, o4:], LANES)], axis=1).astype(BF16)
    kig_rot = jnp.pad(jnp.concatenate([kidx_g[ri // 2:ri], kidx_g[:ri // 2]]), (0, DSA_IDX_DIM - ri))
    uq = w_uq.reshape(DSA_Q_RANK, DSA_HEADS, DSA_QK_DIM)
    w_rope = uq[:, :, :DSA_ROPE_DIM]
    w_nope = jnp.pad(uq[:, :, DSA_ROPE_DIM:], ((0, 0), (0, 0), (0, LANES - DSA_NOPE_DIM)))
    qi = w_qidx.reshape(DSA_Q_RANK, DSA_IDX_HEADS, DSA_IDX_DIM)
    qi_rot = jnp.pad(_rot_half(qi[:, :, :ri]), ((0, 0), (0, 0), (0, DSA_IDX_DIM - ri)))
    wukt = jnp.transpose(w_uk.reshape(DSA_KV_RANK, DSA_HEADS, DSA_NOPE_DIM), (1, 2, 0))
    wukt = jnp.pad(wukt, ((0, 0), (0, LANES - DSA_NOPE_DIM), (0, 0))).astype(BF16)
    tab = _rope_tables(seq)
    nblk = seq // tm
    flat = lambda a: a.reshape(DSA_Q_RANK, -1).astype(BF16)
    tok = lambda c: pl.BlockSpec((tm, c), lambda i: (i, 0))
    return pl.pallas_call(
        _dsa_prep_kernel,
        out_shape=(
            jax.ShapeDtypeStruct((DSA_IDX_HEADS, n, DSA_IDX_DIM), BF16),
            jax.ShapeDtypeStruct((n, LANES), F32),
            jax.ShapeDtypeStruct((DSA_HEADS, n, DSA_LAT), BF16),
            jax.ShapeDtypeStruct((n, DSA_IDX_DIM), BF16),
            jax.ShapeDtypeStruct((n, DSA_LAT), BF16),
        ),
        grid=(n // tm,),
        in_specs=[
            tok(d), _resident((1, d)), _resident((d, _IN_COLS)),
            _resident((1, DSA_Q_RANK)), _resident((1, DSA_KV_RANK)),
            _resident((1, DSA_IDX_DIM)), _resident((1, DSA_IDX_DIM)),
            _resident((DSA_Q_RANK, DSA_HEADS * LANES)), _resident((DSA_Q_RANK, DSA_HEADS * DSA_ROPE_DIM)),
            _resident((DSA_Q_RANK, DSA_HEADS * DSA_ROPE_DIM)),
            _resident((DSA_Q_RANK, DSA_IDX_HEADS * DSA_IDX_DIM)), _resident((DSA_Q_RANK, DSA_IDX_HEADS * DSA_IDX_DIM)),
            _resident((DSA_HEADS, LANES, DSA_KV_RANK)),
            pl.BlockSpec((tm, _TAB_COLS), lambda i: (i % nblk, 0)),
        ],
        out_specs=(
            pl.BlockSpec((DSA_IDX_HEADS, tm, DSA_IDX_DIM), lambda i: (0, i, 0)), tok(LANES),
            pl.BlockSpec((DSA_HEADS, tm, DSA_LAT), lambda i: (0, i, 0)),
            tok(DSA_IDX_DIM), tok(DSA_LAT),
        ),
        compiler_params=_params("parallel"),
        name="dsa_prep",
    )(x, norm_g, w_ext, cq_g[None], ckv_g[None], kidx_g[None], kig_rot[None],
      flat(w_nope), flat(w_rope), flat(_rot_half(w_rope)), flat(qi), flat(qi_rot), wukt, tab)


def _dsa_attn_kernel(qidx_ref, widx_ref, qlat_ref, kidx_ref, klat_ref, o_ref,
                     key_ref, bias_ref, m_ref, l_ref, acc_ref, *, topk, scale):
    i = pl.program_id(1)
    tq, tk = DSA_TQ, DSA_TK
    sub = tk // LANES
    nsteps = (i * tq + tq + tk - 1) // tk
    w_t = widx_ref[...].T
    qpos = i * tq + lax.broadcasted_iota(jnp.int32, (1, tq), 1)
    krow = lax.broadcasted_iota(jnp.int32, (LANES, tq), 0)
    q_idx = qidx_ref[...].reshape(DSA_IDX_HEADS * tq, DSA_IDX_DIM)

    def score_step(jj, _):
        base = pl.multiple_of(jj * tk, tk)
        lg = _dot_nt(kidx_ref[pl.ds(base, tk), :], q_idx)
        for u in range(sub):
            rows = slice(u * LANES, (u + 1) * LANES)
            s = w_t[0:1, :] * jnp.maximum(lg[rows, 0:tq], 0.0)
            for h in range(1, DSA_IDX_HEADS):
                s = s + w_t[h:h + 1, :] * jnp.maximum(lg[rows, h * tq:(h + 1) * tq], 0.0)
            s = jnp.where(s == 0.0, 0.0, s)
            bits = lax.bitcast_convert_type(s, jnp.int32)
            key = bits ^ ((bits >> 31) & 0x7FFFFFFF)
            key = jnp.where(jj * tk + u * LANES + krow <= qpos, key, INT32_MIN)
            key_ref[pl.ds(pl.multiple_of(base + u * LANES, LANES), LANES), :] = key
        return 0

    lax.fori_loop(0, nsteps, score_step, 0)

    nacc = DSA_COUNT_ACCS

    def count_ge(trial):
        def body(jj, acc):
            blk = key_ref[pl.ds(pl.multiple_of(jj * tk, tk), tk), :]
            hit = jnp.where(blk >= trial, 1.0, 0.0)
            return acc + jnp.sum(hit.reshape(tk // (8 * nacc), nacc * 8, tq), axis=0)
        acc = lax.fori_loop(0, nsteps, body, jnp.zeros((nacc * 8, tq), F32))
        return jnp.sum(acc, axis=0, keepdims=True)

    def bit_step(it, r):
        trial = r + jnp.left_shift(jnp.int32(1), 31 - it)
        return jnp.where(count_ge(trial) >= topk, trial, r)

    kth = lax.fori_loop(0, 32, bit_step, jnp.full((1, tq), INT32_MIN, jnp.int32))
    n_ge = count_ge(kth)
    n_gt = count_ge(kth + 1)
    need = topk - n_gt
    valid_min = INT32_MIN + 1
    has_tie = jnp.where((n_ge > topk) & (kth >= valid_min), 1.0, 0.0)
    tie_any = jnp.max(has_tie) > 0.0

    def write_bias(jj, u, sel):
        bias_ref[jj, :, u * LANES:(u + 1) * LANES] = jnp.where(sel, 0.0, MASKED).T

    @pl.when(jnp.logical_not(tie_any))
    def _():
        thr = jnp.maximum(kth, valid_min)

        def body(jj, _):
            for u in range(sub):
                blk = key_ref[pl.ds(pl.multiple_of((jj * sub + u) * LANES, LANES), LANES), :]
                write_bias(jj, u, blk >= thr)
            return 0

        lax.fori_loop(0, nsteps, body, 0)

    @pl.when(tie_any)
    def _():
        r2 = lax.broadcasted_iota(jnp.int32, (LANES, LANES), 0)
        c2 = lax.broadcasted_iota(jnp.int32, (LANES, LANES), 1)
        before = jnp.where(c2 < r2, 1.0, 0.0).astype(BF16)

        def body(jj, run):
            for u in range(sub):
                blk = key_ref[pl.ds(pl.multiple_of((jj * sub + u) * LANES, LANES), LANES), :]
                eq = jnp.where((blk == kth) & (blk >= valid_min), 1.0, 0.0)
                rank = _dot(before, eq.astype(BF16)) + run
                take = jnp.where(rank < need, eq, 0.0)
                write_bias(jj, u, jnp.where(blk > kth, 1.0, take) > 0.0)
                run = run + jnp.sum(eq, axis=0, keepdims=True)
            return run

        lax.fori_loop(0, nsteps, body, jnp.zeros((1, tq), F32))

    ng = DSA_HEAD_GROUPS
    gh = DSA_HEADS // ng
    gr = gh * tq
    qs = [qlat_ref[g * gh:(g + 1) * gh].reshape(gr, DSA_LAT) for g in range(ng)]
    m_ref[...] = jnp.full_like(m_ref, MASKED)
    l_ref[...] = jnp.zeros_like(l_ref)
    acc_ref[...] = jnp.zeros_like(acc_ref)

    def attn_step(jj, _):
        kt = klat_ref[pl.ds(pl.multiple_of(jj * tk, tk), tk), :]
        kv = kt[:, :DSA_KV_RANK]
        bias = bias_ref[jj][None]
        gs = range(ng)
        rows = [slice(g * gr, (g + 1) * gr) for g in gs]
        att = [_dot_nt(qs[g], kt) * scale for g in gs]
        att = [(a.reshape(gh, tq, tk) + bias).reshape(gr, tk) for a in att]
        m_old = [m_ref[rows[g], :] for g in gs]
        m_new = [jnp.maximum(m_old[g], jnp.max(att[g], axis=-1, keepdims=True)) for g in gs]
        p = [jnp.exp(att[g] - m_new[g]) for g in gs]
        alpha = [jnp.exp(m_old[g] - m_new[g]) for g in gs]
        pv = [_dot(p[g].astype(BF16), kv) for g in gs]
        for g in gs:
            l_ref[rows[g], :] = alpha[g] * l_ref[rows[g], :] + jnp.sum(p[g], axis=-1, keepdims=True)
            acc_ref[rows[g], :] = alpha[g] * acc_ref[rows[g], :] + pv[g]
            m_ref[rows[g], :] = m_new[g]
        return 0

    lax.fori_loop(0, nsteps, attn_step, 0)
    o = acc_ref[...] / l_ref[...]
    o_ref[...] = o.reshape(DSA_HEADS, tq, DSA_KV_RANK).astype(o_ref.dtype)


def _dsa_attn(qidx, widx, qlat, kidx, klat, topk):
    _, b, s, _ = qidx.shape
    tq = DSA_TQ
    return pl.pallas_call(
        functools.partial(_dsa_attn_kernel, topk=topk, scale=DSA_QK_DIM ** -0.5),
        out_shape=jax.ShapeDtypeStruct((DSA_HEADS, b, s, DSA_KV_RANK), BF16),
        grid=(b, s // tq),
        in_specs=[
            pl.BlockSpec((DSA_IDX_HEADS, None, tq, DSA_IDX_DIM), lambda bi, i: (0, bi, i, 0)),
            pl.BlockSpec((None, tq, LANES), lambda bi, i: (bi, i, 0)),
            pl.BlockSpec((DSA_HEADS, None, tq, DSA_LAT), lambda bi, i: (0, bi, i, 0)),
            pl.BlockSpec((None, s, DSA_IDX_DIM), lambda bi, i: (bi, 0, 0)),
            pl.BlockSpec((None, s, DSA_LAT), lambda bi, i: (bi, 0, 0)),
        ],
        out_specs=pl.BlockSpec((DSA_HEADS, None, tq, DSA_KV_RANK), lambda bi, i: (0, bi, i, 0)),
        scratch_shapes=[
            pltpu.VMEM((s, tq), jnp.int32),
            pltpu.VMEM((s // DSA_TK, tq, DSA_TK), F32),
            pltpu.VMEM((DSA_HEADS * tq, 1), F32),
            pltpu.VMEM((DSA_HEADS * tq, 1), F32),
            pltpu.VMEM((DSA_HEADS * tq, DSA_KV_RANK), F32),
        ],
        compiler_params=_params("parallel", "arbitrary"),
        name="dsa_attn",
    )(qidx, widx, qlat, kidx, klat)


def _dsa_mixer(x, norm_g, w_in, cq_g, ckv_g, kidx_g, w_uq, w_qidx, w_uk, w_uv, w_out, bsz, seq):
    qidx, widx, qlat, kidx, klat = _dsa_prep(x, norm_g, w_in, cq_g, ckv_g, kidx_g, w_uq, w_qidx, w_uk, bsz, seq)
    topk = min(DSA_TOPK_MAX, seq // 4)
    b3 = lambda a: a.reshape(bsz, seq, -1)
    olat = _dsa_attn(qidx.reshape(DSA_IDX_HEADS, bsz, seq, DSA_IDX_DIM), b3(widx),
                     qlat.reshape(DSA_HEADS, bsz, seq, DSA_LAT), b3(kidx), b3(klat), topk)
    wuv = jnp.transpose(w_uv.reshape(DSA_KV_RANK, DSA_HEADS, DSA_V_DIM), (1, 0, 2)).astype(BF16)
    return "dsa", (olat.reshape(DSA_HEADS, bsz * seq, DSA_KV_RANK), wuv, w_out.astype(BF16))


def kernel(x, ffn1_norm, ffn1_w_gu, ffn1_w_down, mix_norm, ffn2_norm, ffn2_w_gu, ffn2_w_down, gdn_w_in, gdn_conv, gdn_a_log, gdn_dt_bias, gdn_norm, gdn_w_out, sb_w_in, sb_w_out, dsa_w_in, dsa_cq_norm, dsa_ckv_norm, dsa_kidx_norm, dsa_w_uq, dsa_w_qidx, dsa_w_uk, dsa_w_uv, dsa_w_out, final_norm):
    bsz, seq, d = x.shape
    d_ff = ffn1_w_down.shape[1]
    xf = x.reshape(bsz * seq, d)

    def ffn(xf, g, w_gu, w_down, **kw):
        return _ffn(xf, g[None], w_gu[:, :d_ff].astype(BF16), w_gu[:, d_ff:].astype(BF16), w_down.astype(BF16), **kw)

    depth = ffn1_norm.shape[0]
    for i in range(depth):
        kind, j = i % N_MIXERS, i // N_MIXERS
        xf = ffn(xf, ffn1_norm[i], ffn1_w_gu[i], ffn1_w_down[i])
        g = mix_norm[i][None]
        if kind == 0:
            pre, pre_args = _gdn_mixer(xf, g, gdn_w_in[j], gdn_conv[j], gdn_a_log[j], gdn_dt_bias[j], gdn_norm[j],
                                       gdn_w_out[j], bsz, seq)
        elif kind == 1:
            pre, pre_args = _stick_breaking_mixer(xf, g, sb_w_in[j], sb_w_out[j], bsz, seq)
        else:
            pre, pre_args = _dsa_mixer(xf, g, dsa_w_in[j], dsa_cq_norm[j], dsa_ckv_norm[j], dsa_kidx_norm[j],
                                       dsa_w_uq[j], dsa_w_qidx[j], dsa_w_uk[j], dsa_w_uv[j], dsa_w_out[j], bsz, seq)
        xf = ffn(xf, ffn2_norm[i], ffn2_w_gu[i], ffn2_w_down[i], pre=pre, pre_args=pre_args,
                 final_g=final_norm[None] if i == depth - 1 else None)
    return xf.reshape(bsz, seq, d)
```

```python
import functools
import math

import jax
import jax.numpy as jnp
from jax import lax
from jax.experimental import pallas as pl
from jax.experimental.pallas import tpu as pltpu

F32 = jnp.float32
BF16 = jnp.bfloat16

NORM_EPS = 1e-6
ROPE_THETA = 500000.0
V7X_VMEM_LIMIT_BYTES = 56 * 1024 * 1024

N_MIXERS = 3
SB_HEADS = 8
SB_HEAD_DIM = 128


def _params(*sem):
    return pltpu.CompilerParams(dimension_semantics=sem, vmem_limit_bytes=V7X_VMEM_LIMIT_BYTES)


def _dot(a, b):
    return jnp.dot(a, b, preferred_element_type=F32)


def _dot_nt(a, b):
    return lax.dot_general(a, b, (((1,), (1,)), ((), ())), preferred_element_type=F32)


def _rms(x, g):
    return x * lax.rsqrt(jnp.mean(x * x, axis=-1, keepdims=True) + NORM_EPS) * g


def _resident(shape):
    return pl.BlockSpec(shape, lambda *_: (0,) * len(shape), pipeline_mode=pl.Buffered(1))


def _ffn_kernel(*refs, fc, pre, post_norm):
    refs = list(refs)
    x_ref = refs.pop(0)
    x = x_ref[...]
    if pre == "proj":
        o_ref_in, wout_ref = refs.pop(0), refs.pop(0)
        x = x + _dot(o_ref_in[...], wout_ref[...])
    elif pre == "dsa":
        olat_ref, wuv_ref, wout_ref = refs.pop(0), refs.pop(0), refs.pop(0)
        parts = [_dot(olat_ref[h], wuv_ref[h]) for h in range(olat_ref.shape[0])]
        x = x + _dot(jnp.concatenate(parts, axis=-1).astype(BF16), wout_ref[...])
    g_ref, wg_ref, wu_ref, wd_ref = refs[:4]
    o_ref, acc_ref = refs[-2:]
    xn = _rms(x, g_ref[...]).astype(BF16)
    d_ff = wg_ref.shape[1]
    for c in range(d_ff // fc):
        sl = slice(c * fc, (c + 1) * fc)
        gate = _dot(xn, wg_ref[:, sl])
        up = _dot(xn, wu_ref[:, sl])
        h = (gate / (1.0 + jnp.exp(-gate)) * up).astype(BF16)
        y = _dot(h, wd_ref[sl, :])
        if c == 0:
            acc_ref[...] = y
        else:
            acc_ref[...] += y
    y = x + 0.5 * acc_ref[...]
    if post_norm:
        y = _rms(y, refs[4][...])
    o_ref[...] = y


def _ffn(x, g, wg, wu, wd, *, pre=None, pre_args=(), final_g=None, tm=512, fc=256):
    n, d = x.shape
    d_ff = wg.shape[1]
    tok = lambda c: pl.BlockSpec((tm, c), lambda i: (i, 0))
    pre_specs = []
    if pre == "proj":
        o, w_out = pre_args
        pre_specs = [tok(o.shape[1]), _resident(w_out.shape)]
    elif pre == "dsa":
        olat, wuv, w_out = pre_args
        pre_specs = [pl.BlockSpec((olat.shape[0], tm, olat.shape[2]), lambda i: (0, i, 0)),
                     _resident(wuv.shape), _resident(w_out.shape)]
    post = [] if final_g is None else [final_g]
    return pl.pallas_call(
        functools.partial(_ffn_kernel, fc=fc, pre=pre, post_norm=final_g is not None),
        out_shape=jax.ShapeDtypeStruct((n, d), F32),
        grid=(n // tm,),
        in_specs=[tok(d)] + pre_specs + [_resident((1, d)), _resident((d, d_ff)), _resident((d, d_ff)),
                                          _resident((d_ff, d))] + [_resident((1, d))] * len(post),
        out_specs=tok(d),
        scratch_shapes=[pltpu.VMEM((tm, d), F32)],
        compiler_params=_params("parallel"),
        name="ffn",
    )(x, *pre_args, g, wg, wu, wd, *post)


def _norm_proj_kernel(x_ref, g_ref, w_ref, *rest):
    o_ref = rest[-1]
    xn = _rms(x_ref[...], g_ref[...]).astype(BF16)
    y = _dot(xn, w_ref[...])
    if len(rest) == 2:
        y = y * rest[0][...]
    o_ref[...] = y.astype(o_ref.dtype)


def _norm_proj(x, g, w, out_dtype, col_scale=None, *, tm=512):
    n, d = x.shape
    m = w.shape[1]
    extra = [] if col_scale is None else [col_scale]
    return pl.pallas_call(
        _norm_proj_kernel,
        out_shape=jax.ShapeDtypeStruct((n, m), out_dtype),
        grid=(n // tm,),
        in_specs=[pl.BlockSpec((tm, d), lambda i: (i, 0)), _resident((1, d)), _resident((d, m))]
        + [_resident((1, m))] * len(extra),
        out_specs=pl.BlockSpec((tm, m), lambda i: (i, 0)),
        compiler_params=_params("parallel"),
        name="norm_proj",
    )(x, g, w, *extra)


SB_LINEAR_ABOVE = 30.0
SB_HEADS_PER_STEP = 8


def _sb_kernel(q_ref, k_ref, v_ref, o_ref, *, t):
    i = pl.program_id(2)
    d = SB_HEAD_DIM
    nh = SB_HEADS_PER_STEP
    row = lax.broadcasted_iota(jnp.int32, (t, t), 0)
    col = lax.broadcasted_iota(jnp.int32, (t, t), 1)
    tri = jnp.where(row >= col, 1.0, 0.0).astype(BF16)
    earlier = col < row
    qs = [q_ref[:, h * d:(h + 1) * d] for h in range(nh)]

    def tiles(j, st, diag):
        ks = pl.ds(pl.multiple_of(j * t, t), t)
        hs = range(nh)
        zs = [_dot_nt(qs[h], k_ref[ks, h * d:(h + 1) * d]) for h in hs]
        sps = [jnp.where(z > SB_LINEAR_ABOVE, z, jnp.log2(1.0 + jnp.exp2(z))) for z in zs]
        if diag:
            sps = [jnp.where(earlier, sp, 0.0) for sp in sps]
        rins = [_dot(sp.astype(BF16), tri) for sp in sps]
        atts = [jnp.exp2(zs[h] - rins[h] - st[h][0]) for h in hs]
        if diag:
            atts = [jnp.where(earlier, a, 0.0) for a in atts]
        accs = [st[h][1] + _dot(atts[h].astype(BF16), v_ref[ks, h * d:(h + 1) * d]) for h in hs]
        return tuple((st[h][0] + rins[h][:, 0:1], accs[h]) for h in hs)

    state = tiles(i, tuple((jnp.zeros((t, 1), F32), jnp.zeros((t, d), F32)) for _ in range(nh)), True)
    state = lax.fori_loop(0, i, lambda s, st: tiles(i - 1 - s, st, False), state)
    for h in range(nh):
        o_ref[:, h * d:(h + 1) * d] = state[h][1].astype(o_ref.dtype)


def _sb_attention(qkv, *, t=256):
    b, s, _ = qkv.shape
    h, d = SB_HEADS, SB_HEAD_DIM
    nh = SB_HEADS_PER_STEP
    w = nh * d
    ng = h // nh
    return pl.pallas_call(
        functools.partial(_sb_kernel, t=t),
        out_shape=jax.ShapeDtypeStruct((b, s, h * d), BF16),
        grid=(b, ng, s // t),
        in_specs=[
            pl.BlockSpec((None, t, w), lambda bi, gi, i: (bi, i, gi)),
            pl.BlockSpec((None, s, w), lambda bi, gi, i: (bi, 0, ng + gi)),
            pl.BlockSpec((None, s, w), lambda bi, gi, i: (bi, 0, 2 * ng + gi)),
        ],
        out_specs=pl.BlockSpec((None, t, w), lambda bi, gi, i: (bi, i, gi)),
        compiler_params=_params("parallel", "parallel", "arbitrary"),
        name="sb_attention",
    )(qkv, qkv, qkv)


def _stick_breaking_mixer(x, norm_g, w_in, w_out, bsz, seq):
    nq = SB_HEADS * SB_HEAD_DIM
    col_scale = jnp.concatenate([jnp.full((nq,), SB_HEAD_DIM ** -0.5 * math.log2(math.e), F32),
                                 jnp.ones((2 * nq,), F32)])[None]
    qkv = _norm_proj(x, norm_g, w_in.astype(BF16), BF16, col_scale)
    o = _sb_attention(qkv.reshape(bsz, seq, -1))
    return "proj", (o.reshape(bsz * seq, -1), w_out.astype(BF16))


GDN_HEADS = 8
GDN_HEAD_DIM = 128
GDN_DIM = GDN_HEADS * GDN_HEAD_DIM
GDN_CHUNK = 64
GDN_BLOCK = 128
GDN_CONV = 4
CONV_TAIL_ROWS = 8


def _split2(a):
    hi = a.astype(BF16)
    return hi, (a - hi.astype(F32)).astype(BF16)


def _dot_exact_rhs(a, b01):
    h0 = a.astype(BF16)
    r1 = a - h0.astype(F32)
    h1 = r1.astype(BF16)
    h2 = (r1 - h1.astype(F32)).astype(BF16)
    return _dot(b01, h0) + (_dot(b01, h1) + _dot(b01, h2))


def _sigmoid(x):
    return 1.0 / (1.0 + jnp.exp(-x))


def _softplus(x):
    return jnp.maximum(x, 0.0) + jnp.log1p(jnp.exp(-jnp.abs(x)))


def _gdn_kernel(main_ref, small_ref, conv_ref, alog_ref, dt_ref, ng_ref, o_ref, state_ref, tail_ref):
    t = GDN_BLOCK
    c = GDN_CHUNK
    d = GDN_HEAD_DIM
    nqkv = 3 * GDN_DIM

    @pl.when(pl.program_id(1) == 0)
    def _():
        state_ref[...] = jnp.zeros_like(state_ref)
        tail_ref[:CONV_TAIL_ROWS, :] = jnp.zeros((CONV_TAIL_ROWS, tail_ref.shape[1]), F32)

    cur = main_ref[:, :nqkv]
    tail_ref[CONV_TAIL_ROWS:, :] = cur
    conv = cur * conv_ref[GDN_CONV - 1:GDN_CONV, :]
    for k in range(1, GDN_CONV):
        conv = conv + tail_ref[CONV_TAIL_ROWS - k:CONV_TAIL_ROWS - k + t, :] * conv_ref[GDN_CONV - 1 - k:GDN_CONV - k, :]
    tail_ref[:CONV_TAIL_ROWS, :] = cur[t - CONV_TAIL_ROWS:, :]
    qkv = conv * _sigmoid(conv)

    small = small_ref[...]
    beta_all = _sigmoid(small)
    g_all = -jnp.exp(alog_ref[...]) * _softplus(small + dt_ref[...])
    r2 = lax.broadcasted_iota(jnp.int32, (t, t), 0)
    c2 = lax.broadcasted_iota(jnp.int32, (t, t), 1)
    same_chunk = (r2 // c) == (c2 // c)
    tril_bd = jnp.where(same_chunk & (c2 <= r2), 1.0, 0.0).astype(BF16)
    gc_all = _dot_exact_rhs(g_all, tril_bd)
    gc_t = gc_all.T

    ng = ng_ref[...]
    nch = t // c
    nh = GDN_HEADS
    nb = nch * nh
    npair = nb // 2

    def stack(x, col0):
        return jnp.stack([x[ch * c:(ch + 1) * c, col0 + h * d:col0 + (h + 1) * d]
                          for ch in range(nch) for h in range(nh)])

    def cols(x, col0):
        return [x[ch * c:(ch + 1) * c, col0 + h:col0 + h + 1] for ch in range(nch) for h in range(nh)]

    q = stack(qkv, 0)
    k = stack(qkv, GDN_DIM)
    v = stack(qkv, 2 * GDN_DIM)
    q = q * lax.rsqrt(jnp.sum(q * q, axis=-1, keepdims=True) + NORM_EPS) * (d ** -0.5)
    k = k * lax.rsqrt(jnp.sum(k * k, axis=-1, keepdims=True) + NORM_EPS)
    beta_c = cols(beta_all, 0)
    gc_c = cols(gc_all, nh)
    beta = jnp.stack(beta_c)
    gc_col = jnp.stack(gc_c)
    gc_last = gc_col[:, c - 1:c, :]

    ri = lax.broadcasted_iota(jnp.int32, (c, 2 * c), 0)
    li = lax.broadcasted_iota(jnp.int32, (c, 2 * c), 1)
    left = (li < c)[None]
    cj = jnp.where(li < c, li, li - c)
    incl = (cj <= ri)[None]
    strict = (cj < ri)[None]
    eye = jnp.where(cj == ri, 1.0, 0.0)[None]

    def pair_cols(xs):
        return jnp.where(left, jnp.stack(xs[0::2]), jnp.stack(xs[1::2]))

    def pair_lanes(x):
        return jnp.concatenate([jnp.stack([x[2 * m] for m in range(npair)]),
                                jnp.stack([x[2 * m + 1] for m in range(npair)])], axis=-1)

    def block_diag(x):
        zero = jnp.zeros_like(x)
        return jnp.concatenate([jnp.where(left, x, zero), jnp.where(left, zero, x)], axis=-2)

    def pair_dot3(a_parts, b_parts):
        ah, al = a_parts
        bh, bl = b_parts
        lhs = jnp.concatenate([ah, ah, al], axis=-1)
        rhs = jnp.concatenate([block_diag(bh), block_diag(bl), block_diag(bh)], axis=-2)
        return jnp.einsum("bik,bkj->bij", lhs, rhs, preferred_element_type=F32)

    def pair_apply(a_parts, b_parts):
        ah, al = a_parts
        bh, bl = b_parts
        rhs = jnp.concatenate([bh, bl, bh], axis=-2)
        out = []
        for keep_left in (True, False):
            mh = jnp.where(left, ah, jnp.zeros_like(ah)) if keep_left else jnp.where(left, jnp.zeros_like(ah), ah)
            ml = jnp.where(left, al, jnp.zeros_like(al)) if keep_left else jnp.where(left, jnp.zeros_like(al), al)
            out.append(jnp.einsum("bik,bkj->bij", jnp.concatenate([mh, mh, ml], axis=-1), rhs,
                                  preferred_element_type=F32))
        return out

    def pair_rows(x):
        return jnp.concatenate([jnp.stack([x[2 * m] for m in range(npair)]),
                                jnp.stack([x[2 * m + 1] for m in range(npair)])], axis=-2)

    gcr = [gc_t[nh + h:nh + h + 1, ch * c:(ch + 1) * c] for ch in range(nch) for h in range(nh)]
    gc_row = jnp.stack([jnp.concatenate([gcr[2 * m], gcr[2 * m + 1]], axis=-1) for m in range(npair)])
    decay = jnp.where(incl, jnp.exp(jnp.where(incl, pair_cols(gc_c) - gc_row, 0.0)), 0.0)
    kb = k.astype(BF16)
    kk = pair_lanes(jnp.einsum("bik,bjk->bij", kb, kb, preferred_element_type=F32))
    neg_l = jnp.where(strict, -(pair_cols(beta_c) * kk * decay), 0.0)
    inv = eye + neg_l
    pw_parts = _split2(neg_l)
    for _ in range(int(math.log2(c)) - 1):
        pw_parts = _split2(pair_dot3(pw_parts, pw_parts))
        inv = inv + pair_dot3(_split2(inv), pw_parts)
    e_gc = jnp.exp(gc_col)
    rhs = jnp.concatenate([v * beta, k * (beta * e_gc)], axis=-1)
    sol_e, sol_o = pair_apply(_split2(inv), _split2(pair_rows(rhs)))
    qk = pair_lanes(jnp.einsum("bik,bjk->bij", q.astype(BF16), kb, preferred_element_type=F32)) * decay
    qkb = qk.astype(BF16)
    zero_qk = jnp.zeros_like(qkb)
    qk_e = jnp.where(left, qkb, zero_qk)
    qk_o = jnp.where(left, zero_qk, qkb)
    q_dec = (q * e_gc).astype(BF16)
    k_dec = (k * jnp.exp(gc_last - gc_col)).astype(BF16)
    g_tot = jnp.exp(gc_last)

    state = state_ref[...]
    hp = nh // 2
    for ch in range(nch):
        hs = slice(ch * nh, (ch + 1) * nh)
        ps = slice(ch * hp, (ch + 1) * hp)
        sb = state.astype(BF16)
        sol = jnp.stack([(sol_e, sol_o)[h % 2][ch * hp + h // 2] for h in range(nh)])
        u = sol[:, :, :d] - jnp.einsum("hck,hkv->hcv", sol[:, :, d:].astype(BF16), sb, preferred_element_type=F32)
        ub = u.astype(BF16)
        u2 = jnp.concatenate([jnp.stack([ub[2 * m] for m in range(hp)]),
                              jnp.stack([ub[2 * m + 1] for m in range(hp)])], axis=-2)
        intra_e = jnp.einsum("pcj,pjv->pcv", qk_e[ps], u2, preferred_element_type=F32)
        intra_o = jnp.einsum("pcj,pjv->pcv", qk_o[ps], u2, preferred_element_type=F32)
        intra = jnp.stack([(intra_e, intra_o)[h % 2][h // 2] for h in range(nh)])
        o = jnp.einsum("hck,hkv->hcv", q_dec[hs], sb, preferred_element_type=F32) + intra
        kdu = jnp.stack([lax.dot_general(k_dec[ch * nh + h], ub[h], (((0,), (0,)), ((), ())),
                                         preferred_element_type=F32) for h in range(nh)])
        state = state * g_tot[hs] + kdu
        o = o * lax.rsqrt(jnp.mean(o * o, axis=-1, keepdims=True) + NORM_EPS) * ng
        rows = slice(ch * c, (ch + 1) * c)
        z = main_ref[rows, nqkv:nqkv + GDN_DIM]
        o_all = jnp.concatenate([o[h] for h in range(nh)], axis=-1)
        o_ref[rows, :] = (o_all * (z * _sigmoid(z))).astype(o_ref.dtype)
    state_ref[...] = state


def _gdn_core(proj, conv_w, alog_pad, dt_pad, norm_g):
    b, s, _ = proj.shape
    t = GDN_BLOCK
    return pl.pallas_call(
        _gdn_kernel,
        out_shape=jax.ShapeDtypeStruct((b, s, GDN_DIM), BF16),
        grid=(b, s // t),
        in_specs=[
            pl.BlockSpec((None, t, 4 * GDN_DIM), lambda bi, i: (bi, i, 0)),
            pl.BlockSpec((None, t, 128), lambda bi, i: (bi, i, 4 * GDN_DIM // 128)),
            _resident((GDN_CONV, 3 * GDN_DIM)),
            _resident((1, 128)),
            _resident((1, 128)),
            _resident((1, GDN_HEAD_DIM)),
        ],
        out_specs=pl.BlockSpec((None, t, GDN_DIM), lambda bi, i: (bi, i, 0)),
        scratch_shapes=[
            pltpu.VMEM((GDN_HEADS, GDN_HEAD_DIM, GDN_HEAD_DIM), F32),
            pltpu.VMEM((CONV_TAIL_ROWS + GDN_BLOCK, 3 * GDN_DIM), F32),
        ],
        compiler_params=_params("parallel", "arbitrary"),
        name="gdn_core",
    )(proj, proj, conv_w, alog_pad, dt_pad, norm_g)


def _gdn_mixer(x, norm_g, w_in, conv_w, a_log, dt_bias, head_norm, w_out, bsz, seq):
    pad = 128 - 2 * GDN_HEADS
    w = jnp.pad(w_in, ((0, 0), (0, pad))).astype(BF16)
    proj = _norm_proj(x, norm_g, w, F32)
    zeros = jnp.zeros((GDN_HEADS,), F32)
    alog_pad = jnp.pad(jnp.concatenate([zeros, a_log]), (0, pad))[None]
    dt_pad = jnp.pad(jnp.concatenate([zeros, dt_bias]), (0, pad))[None]
    o = _gdn_core(proj.reshape(bsz, seq, -1), conv_w, alog_pad, dt_pad, head_norm[None])
    return "proj", (o.reshape(bsz * seq, -1), w_out.astype(BF16))


DSA_HEADS = 8
DSA_Q_RANK = 384
DSA_KV_RANK = 256
DSA_ROPE_DIM = 32
DSA_NOPE_DIM = 96
DSA_QK_DIM = DSA_ROPE_DIM + DSA_NOPE_DIM
DSA_V_DIM = 128
DSA_IDX_HEADS = 8
DSA_IDX_DIM = 64
DSA_IDX_ROPE = DSA_IDX_DIM // 4
DSA_TOPK_MAX = 256
DSA_LAT = DSA_KV_RANK + DSA_ROPE_DIM
LANES = 128
DSA_TQ = 128
DSA_TK = 512
DSA_HEAD_GROUPS = 4
DSA_COUNT_ACCS = 4
MASKED = -1e30
INT32_MIN = -2 ** 31

_IN_CQ = 0
_IN_CKV = 384
_IN_KROPE = 640
_IN_KROPE_ROT = 768
_IN_KIDX = 896
_IN_KIDX_ROT = 1024
_IN_WIDX = 1152
_IN_COLS = 1280
_TAB_COS = 0
_TAB_SIN = DSA_HEADS * DSA_ROPE_DIM
_TAB_ICOS = 2 * DSA_HEADS * DSA_ROPE_DIM
_TAB_ISIN = _TAB_ICOS + DSA_IDX_HEADS * DSA_IDX_DIM
_TAB_COLS = _TAB_ISIN + DSA_IDX_HEADS * DSA_IDX_DIM


def _dsa_prep_kernel(x_ref, mg_ref, win_ref, cqg_ref, ckvg_ref, kig_ref, kigr_ref, wnope_ref, wrope_ref,
                     wroper_ref, wqi_ref, wqir_ref, wukt_ref, tab_ref,
                     qidx_ref, widx_ref, qlat_ref, kidx_ref, klat_ref):
    h = _rms(x_ref[...], mg_ref[...]).astype(BF16)
    proj = _dot(h, win_ref[...])
    cq = _rms(proj[:, _IN_CQ:_IN_CQ + DSA_Q_RANK], cqg_ref[...]).astype(BF16)
    ckv = _rms(proj[:, _IN_CKV:_IN_CKV + DSA_KV_RANK], ckvg_ref[...])
    tab = tab_ref[...]
    r, di = DSA_ROPE_DIM, DSA_IDX_DIM
    k_rope = (proj[:, _IN_KROPE:_IN_KROPE + r] * tab[:, _TAB_COS:_TAB_COS + r]
              + proj[:, _IN_KROPE_ROT:_IN_KROPE_ROT + r] * tab[:, _TAB_SIN:_TAB_SIN + r])
    klat_ref[:, :DSA_KV_RANK] = ckv.astype(BF16)
    klat_ref[:, DSA_KV_RANK:] = k_rope.astype(BF16)
    kraw = proj[:, _IN_KIDX:_IN_KIDX + di]
    inv = lax.rsqrt(jnp.mean(kraw * kraw, axis=-1, keepdims=True) + NORM_EPS)
    ki = kraw * inv * kig_ref[...]
    kir = proj[:, _IN_KIDX_ROT:_IN_KIDX_ROT + di] * inv * kigr_ref[...]
    kidx_ref[...] = (ki * tab[:, _TAB_ICOS:_TAB_ICOS + di] + kir * tab[:, _TAB_ISIN:_TAB_ISIN + di]).astype(BF16)
    widx_ref[...] = proj[:, _IN_WIDX:_IN_WIDX + LANES] * (DSA_IDX_HEADS ** -0.5 * DSA_IDX_DIM ** -0.5)
    qi = (_dot(cq, wqi_ref[...]) * tab[:, _TAB_ICOS:_TAB_ISIN]
          + _dot(cq, wqir_ref[...]) * tab[:, _TAB_ISIN:_TAB_COLS])
    for hh in range(DSA_IDX_HEADS):
        qidx_ref[hh] = qi[:, hh * di:(hh + 1) * di].astype(BF16)
    q_rope = (_dot(cq, wrope_ref[...]) * tab[:, _TAB_COS:_TAB_SIN]
              + _dot(cq, wroper_ref[...]) * tab[:, _TAB_SIN:_TAB_ICOS])
    q_nope = _dot(cq, wnope_ref[...]).astype(BF16)
    for hh in range(DSA_HEADS):
        q_abs = _dot(q_nope[:, hh * LANES:(hh + 1) * LANES], wukt_ref[hh])
        qlat_ref[hh, :, :DSA_KV_RANK] = q_abs.astype(BF16)
        qlat_ref[hh, :, DSA_KV_RANK:] = q_rope[:, hh * r:(hh + 1) * r].astype(BF16)


def _rot_half(w):
    half = w.shape[-1] // 2
    return jnp.concatenate([-w[..., half:], w[..., :half]], axis=-1)


def _rope_tables(seq):
    pos = jnp.arange(seq, dtype=F32)[:, None]

    def cos_sin(r):
        half = r // 2
        inv_freq = ROPE_THETA ** (-jnp.arange(half, dtype=F32) * (2.0 / r))
        ang = pos * inv_freq[None, :]
        return jnp.cos(ang), jnp.sin(ang)

    c, s = cos_sin(DSA_ROPE_DIM)
    ic, isn = cos_sin(DSA_IDX_ROPE)
    rest = DSA_IDX_DIM - DSA_IDX_ROPE
    icos = jnp.concatenate([ic, ic, jnp.ones((seq, rest), F32)], axis=-1)
    isin = jnp.concatenate([isn, isn, jnp.zeros((seq, rest), F32)], axis=-1)
    return jnp.concatenate([jnp.tile(jnp.concatenate([c, c], -1), (1, DSA_HEADS)),
                            jnp.tile(jnp.concatenate([s, s], -1), (1, DSA_HEADS)),
                            jnp.tile(icos, (1, DSA_IDX_HEADS)), jnp.tile(isin, (1, DSA_IDX_HEADS))], axis=-1)


def _dsa_prep(x, norm_g, w_in, cq_g, ckv_g, kidx_g, w_uq, w_qidx, w_uk, bsz, seq, *, tm=512):
    n, d = x.shape
    o1 = DSA_Q_RANK
    o2 = o1 + DSA_KV_RANK
    o3 = o2 + DSA_ROPE_DIM
    o4 = o3 + DSA_IDX_DIM
    ri = DSA_IDX_ROPE

    def padc(w, cols):
        return jnp.pad(w, ((0, 0), (0, cols - w.shape[1])))

    w_kidx = w_in[:, o3:o4]
    w_ext = jnp.concatenate([
        w_in[:, :o2],
        padc(w_in[:, o2:o3], LANES), padc(_rot_half(w_in[:, o2:o3]), LANES),
        padc(w_kidx, LANES), padc(_rot_half(w_kidx[:, :ri]), LANES),
        padc(w_in[:, o4:], LANES)], axis=1).astype(BF16)
    kig_rot = jnp.pad(jnp.concatenate([kidx_g[ri // 2:ri], kidx_g[:ri // 2]]), (0, DSA_IDX_DIM - ri))
    uq = w_uq.reshape(DSA_Q_RANK, DSA_HEADS, DSA_QK_DIM)
    w_rope = uq[:, :, :DSA_ROPE_DIM]
    w_nope = jnp.pad(uq[:, :, DSA_ROPE_DIM:], ((0, 0), (0, 0), (0, LANES - DSA_NOPE_DIM)))
    qi = w_qidx.reshape(DSA_Q_RANK, DSA_IDX_HEADS, DSA_IDX_DIM)
    qi_rot = jnp.pad(_rot_half(qi[:, :, :ri]), ((0, 0), (0, 0), (0, DSA_IDX_DIM - ri)))
    wukt = jnp.transpose(w_uk.reshape(DSA_KV_RANK, DSA_HEADS, DSA_NOPE_DIM), (1, 2, 0))
    wukt = jnp.pad(wukt, ((0, 0), (0, LANES - DSA_NOPE_DIM), (0, 0))).astype(BF16)
    tab = _rope_tables(seq)
    nblk = seq // tm
    flat = lambda a: a.reshape(DSA_Q_RANK, -1).astype(BF16)
    tok = lambda c: pl.BlockSpec((tm, c), lambda i: (i, 0))
    return pl.pallas_call(
        _dsa_prep_kernel,
        out_shape=(
            jax.ShapeDtypeStruct((DSA_IDX_HEADS, n, DSA_IDX_DIM), BF16),
            jax.ShapeDtypeStruct((n, LANES), F32),
            jax.ShapeDtypeStruct((DSA_HEADS, n, DSA_LAT), BF16),
            jax.ShapeDtypeStruct((n, DSA_IDX_DIM), BF16),
            jax.ShapeDtypeStruct((n, DSA_LAT), BF16),
        ),
        grid=(n // tm,),
        in_specs=[
            tok(d), _resident((1, d)), _resident((d, _IN_COLS)),
            _resident((1, DSA_Q_RANK)), _resident((1, DSA_KV_RANK)),
            _resident((1, DSA_IDX_DIM)), _resident((1, DSA_IDX_DIM)),
            _resident((DSA_Q_RANK, DSA_HEADS * LANES)), _resident((DSA_Q_RANK, DSA_HEADS * DSA_ROPE_DIM)),
            _resident((DSA_Q_RANK, DSA_HEADS * DSA_ROPE_DIM)),
            _resident((DSA_Q_RANK, DSA_IDX_HEADS * DSA_IDX_DIM)), _resident((DSA_Q_RANK, DSA_IDX_HEADS * DSA_IDX_DIM)),
            _resident((DSA_HEADS, LANES, DSA_KV_RANK)),
            pl.BlockSpec((tm, _TAB_COLS), lambda i: (i % nblk, 0)),
        ],
        out_specs=(
            pl.BlockSpec((DSA_IDX_HEADS, tm, DSA_IDX_DIM), lambda i: (0, i, 0)), tok(LANES),
            pl.BlockSpec((DSA_HEADS, tm, DSA_LAT), lambda i: (0, i, 0)),
            tok(DSA_IDX_DIM), tok(DSA_LAT),
        ),
        compiler_params=_params("parallel"),
        name="dsa_prep",
    )(x, norm_g, w_ext, cq_g[None], ckv_g[None], kidx_g[None], kig_rot[None],
      flat(w_nope), flat(w_rope), flat(_rot_half(w_rope)), flat(qi), flat(qi_rot), wukt, tab)


def _dsa_attn_kernel(qidx_ref, widx_ref, qlat_ref, kidx_ref, klat_ref, o_ref,
                     key_ref, bias_ref, m_ref, l_ref, acc_ref, *, topk, scale):
    i = pl.program_id(1)
    tq, tk = DSA_TQ, DSA_TK
    sub = tk // LANES
    nsteps = (i * tq + tq + tk - 1) // tk
    w_t = widx_ref[...].T
    qpos = i * tq + lax.broadcasted_iota(jnp.int32, (1, tq), 1)
    krow = lax.broadcasted_iota(jnp.int32, (LANES, tq), 0)
    q_idx = qidx_ref[...].reshape(DSA_IDX_HEADS * tq, DSA_IDX_DIM)

    def score_step(jj, _):
        base = pl.multiple_of(jj * tk, tk)
        lg = _dot_nt(kidx_ref[pl.ds(base, tk), :], q_idx)
        for u in range(sub):
            rows = slice(u * LANES, (u + 1) * LANES)
            s = w_t[0:1, :] * jnp.maximum(lg[rows, 0:tq], 0.0)
            for h in range(1, DSA_IDX_HEADS):
                s = s + w_t[h:h + 1, :] * jnp.maximum(lg[rows, h * tq:(h + 1) * tq], 0.0)
            s = jnp.where(s == 0.0, 0.0, s)
            bits = lax.bitcast_convert_type(s, jnp.int32)
            key = bits ^ ((bits >> 31) & 0x7FFFFFFF)
            key = jnp.where(jj * tk + u * LANES + krow <= qpos, key, INT32_MIN)
            key_ref[pl.ds(pl.multiple_of(base + u * LANES, LANES), LANES), :] = key
        return 0

    lax.fori_loop(0, nsteps, score_step, 0)

    nacc = DSA_COUNT_ACCS

    def count_ge(trial):
        def body(jj, acc):
            blk = key_ref[pl.ds(pl.multiple_of(jj * tk, tk), tk), :]
            hit = jnp.where(blk >= trial, 1.0, 0.0)
            return acc + jnp.sum(hit.reshape(tk // (8 * nacc), nacc * 8, tq), axis=0)
        acc = lax.fori_loop(0, nsteps, body, jnp.zeros((nacc * 8, tq), F32))
        return jnp.sum(acc, axis=0, keepdims=True)

    def bit_step(it, r):
        trial = r + jnp.left_shift(jnp.int32(1), 31 - it)
        return jnp.where(count_ge(trial) >= topk, trial, r)

    kth = lax.fori_loop(0, 32, bit_step, jnp.full((1, tq), INT32_MIN, jnp.int32))
    n_ge = count_ge(kth)
    n_gt = count_ge(kth + 1)
    need = topk - n_gt
    valid_min = INT32_MIN + 1
    has_tie = jnp.where((n_ge > topk) & (kth >= valid_min), 1.0, 0.0)
    tie_any = jnp.max(has_tie) > 0.0

    def write_bias(jj, u, sel):
        bias_ref[jj, :, u * LANES:(u + 1) * LANES] = jnp.where(sel, 0.0, MASKED).T

    @pl.when(jnp.logical_not(tie_any))
    def _():
        thr = jnp.maximum(kth, valid_min)

        def body(jj, _):
            for u in range(sub):
                blk = key_ref[pl.ds(pl.multiple_of((jj * sub + u) * LANES, LANES), LANES), :]
                write_bias(jj, u, blk >= thr)
            return 0

        lax.fori_loop(0, nsteps, body, 0)

    @pl.when(tie_any)
    def _():
        r2 = lax.broadcasted_iota(jnp.int32, (LANES, LANES), 0)
        c2 = lax.broadcasted_iota(jnp.int32, (LANES, LANES), 1)
        before = jnp.where(c2 < r2, 1.0, 0.0).astype(BF16)

        def body(jj, run):
            for u in range(sub):
                blk = key_ref[pl.ds(pl.multiple_of((jj * sub + u) * LANES, LANES), LANES), :]
                eq = jnp.where((blk == kth) & (blk >= valid_min), 1.0, 0.0)
                rank = _dot(before, eq.astype(BF16)) + run
                take = jnp.where(rank < need, eq, 0.0)
                write_bias(jj, u, jnp.where(blk > kth, 1.0, take) > 0.0)
                run = run + jnp.sum(eq, axis=0, keepdims=True)
            return run

        lax.fori_loop(0, nsteps, body, jnp.zeros((1, tq), F32))

    ng = DSA_HEAD_GROUPS
    gh = DSA_HEADS // ng
    gr = gh * tq
    qs = [qlat_ref[g * gh:(g + 1) * gh].reshape(gr, DSA_LAT) for g in range(ng)]
    m_ref[...] = jnp.full_like(m_ref, MASKED)
    l_ref[...] = jnp.zeros_like(l_ref)
    acc_ref[...] = jnp.zeros_like(acc_ref)

    def attn_step(jj, _):
        kt = klat_ref[pl.ds(pl.multiple_of(jj * tk, tk), tk), :]
        kv = kt[:, :DSA_KV_RANK]
        bias = bias_ref[jj][None]
        gs = range(ng)
        rows = [slice(g * gr, (g + 1) * gr) for g in gs]
        att = [_dot_nt(qs[g], kt) * scale for g in gs]
        att = [(a.reshape(gh, tq, tk) + bias).reshape(gr, tk) for a in att]
        m_old = [m_ref[rows[g], :] for g in gs]
        m_new = [jnp.maximum(m_old[g], jnp.max(att[g], axis=-1, keepdims=True)) for g in gs]
        p = [jnp.exp(att[g] - m_new[g]) for g in gs]
        alpha = [jnp.exp(m_old[g] - m_new[g]) for g in gs]
        pv = [_dot(p[g].astype(BF16), kv) for g in gs]
        for g in gs:
            l_ref[rows[g], :] = alpha[g] * l_ref[rows[g], :] + jnp.sum(p[g], axis=-1, keepdims=True)
            acc_ref[rows[g], :] = alpha[g] * acc_ref[rows[g], :] + pv[g]
            m_ref[rows[g], :] = m_new[g]
        return 0

    lax.fori_loop(0, nsteps, attn_step, 0)
    o = acc_ref[...] / l_ref[...]
    o_ref[...] = o.reshape(DSA_HEADS, tq, DSA_KV_RANK).astype(o_ref.dtype)


def _dsa_attn(qidx, widx, qlat, kidx, klat, topk):
    _, b, s, _ = qidx.shape
    tq = DSA_TQ
    return pl.pallas_call(
        functools.partial(_dsa_attn_kernel, topk=topk, scale=DSA_QK_DIM ** -0.5),
        out_shape=jax.ShapeDtypeStruct((DSA_HEADS, b, s, DSA_KV_RANK), BF16),
        grid=(b, s // tq),
        in_specs=[
            pl.BlockSpec((DSA_IDX_HEADS, None, tq, DSA_IDX_DIM), lambda bi, i: (0, bi, i, 0)),
            pl.BlockSpec((None, tq, LANES), lambda bi, i: (bi, i, 0)),
            pl.BlockSpec((DSA_HEADS, None, tq, DSA_LAT), lambda bi, i: (0, bi, i, 0)),
            pl.BlockSpec((None, s, DSA_IDX_DIM), lambda bi, i: (bi, 0, 0)),
            pl.BlockSpec((None, s, DSA_LAT), lambda bi, i: (bi, 0, 0)),
        ],
        out_specs=pl.BlockSpec((DSA_HEADS, None, tq, DSA_KV_RANK), lambda bi, i: (0, bi, i, 0)),
        scratch_shapes=[
            pltpu.VMEM((s, tq), jnp.int32),
            pltpu.VMEM((s // DSA_TK, tq, DSA_TK), F32),
            pltpu.VMEM((DSA_HEADS * tq, 1), F32),
            pltpu.VMEM((DSA_HEADS * tq, 1), F32),
            pltpu.VMEM((DSA_HEADS * tq, DSA_KV_RANK), F32),
        ],
        compiler_params=_params("parallel", "arbitrary"),
        name="dsa_attn",
    )(qidx, widx, qlat, kidx, klat)


def _dsa_mixer(x, norm_g, w_in, cq_g, ckv_g, kidx_g, w_uq, w_qidx, w_uk, w_uv, w_out, bsz, seq):
    qidx, widx, qlat, kidx, klat = _dsa_prep(x, norm_g, w_in, cq_g, ckv_g, kidx_g, w_uq, w_qidx, w_uk, bsz, seq)
    topk = min(DSA_TOPK_MAX, seq // 4)
    b3 = lambda a: a.reshape(bsz, seq, -1)
    olat = _dsa_attn(qidx.reshape(DSA_IDX_HEADS, bsz, seq, DSA_IDX_DIM), b3(widx),
                     qlat.reshape(DSA_HEADS, bsz, seq, DSA_LAT), b3(kidx), b3(klat), topk)
    wuv = jnp.transpose(w_uv.reshape(DSA_KV_RANK, DSA_HEADS, DSA_V_DIM), (1, 0, 2)).astype(BF16)
    return "dsa", (olat.reshape(DSA_HEADS, bsz * seq, DSA_KV_RANK), wuv, w_out.astype(BF16))


def kernel(x, ffn1_norm, ffn1_w_gu, ffn1_w_down, mix_norm, ffn2_norm, ffn2_w_gu, ffn2_w_down, gdn_w_in, gdn_conv, gdn_a_log, gdn_dt_bias, gdn_norm, gdn_w_out, sb_w_in, sb_w_out, dsa_w_in, dsa_cq_norm, dsa_ckv_norm, dsa_kidx_norm, dsa_w_uq, dsa_w_qidx, dsa_w_uk, dsa_w_uv, dsa_w_out, final_norm):
    bsz, seq, d = x.shape
    d_ff = ffn1_w_down.shape[1]
    xf = x.reshape(bsz * seq, d)

    def ffn(xf, g, w_gu, w_down, **kw):
        return _ffn(xf, g[None], w_gu[:, :d_ff].astype(BF16), w_gu[:, d_ff:].astype(BF16), w_down.astype(BF16), **kw)

    depth = ffn1_norm.shape[0]
    for i in range(depth):
        kind, j = i % N_MIXERS, i // N_MIXERS
        xf = ffn(xf, ffn1_norm[i], ffn1_w_gu[i], ffn1_w_down[i])
        g = mix_norm[i][None]
        if kind == 0:
            pre, pre_args = _gdn_mixer(xf, g, gdn_w_in[j], gdn_conv[j], gdn_a_log[j], gdn_dt_bias[j], gdn_norm[j],
                                       gdn_w_out[j], bsz, seq)
        elif kind == 1:
            pre, pre_args = _stick_breaking_mixer(xf, g, sb_w_in[j], sb_w_out[j], bsz, seq)
        else:
            pre, pre_args = _dsa_mixer(xf, g, dsa_w_in[j], dsa_cq_norm[j], dsa_ckv_norm[j], dsa_kidx_norm[j],
                                       dsa_w_uq[j], dsa_w_qidx[j], dsa_w_uk[j], dsa_w_uv[j], dsa_w_out[j], bsz, seq)
        xf = ffn(xf, ffn2_norm[i], ffn2_w_gu[i], ffn2_w_down[i], pre=pre, pre_args=pre_args,
                 final_g=final_norm[None] if i == depth - 1 else None)
    return xf.reshape(bsz, seq, d)
```
